```python
import math
import jax, jax.numpy as jnp
from jax import lax
import numpy as np

D_MODEL = 1024
BATCH = 8
SEQ = 2048
DEPTH = 4

N_A_LAYERS = DEPTH // 2
N_B_LAYERS = DEPTH - N_A_LAYERS
D_FF = 256 * ((8 * D_MODEL // 3 + 255) // 256)
CONV_WIDTH = 3
HEAD_DIM = 64
N_HEADS = D_MODEL // (2 * HEAD_DIM)
Q_BLOCK = 128
EPS = 1e-6
LAMBDA_STD = 0.1

kernel_name = "yoco_shortconv_diffattn_macaron"


def rms_norm(x, g):
    xf = x.astype(jnp.float32)
    y = xf * lax.rsqrt(jnp.mean(xf * xf, axis=-1, keepdims=True) + EPS)
    return (y * g.astype(jnp.float32)).astype(x.dtype)


def swiglu(h, w_gate, w_up, w_down):
    return (jax.nn.silu(h @ w_gate) * (h @ w_up)) @ w_down


def short_conv_mixer(h, w_in, w_conv, w_out):
    gate_b, gate_c, u = jnp.split(h @ w_in, 3, axis=-1)
    z = lax.conv_general_dilated(
        gate_c * u, w_conv[:, None, :], window_strides=(1,),
        padding=[(CONV_WIDTH - 1, 0)], dimension_numbers=("NWC", "WIO", "NWC"),
        feature_group_count=D_MODEL)
    return (gate_b * z) @ w_out


def shared_kv(x, g_kv, w_kv):
    b, s, _ = x.shape
    k, v = jnp.split(rms_norm(x, g_kv) @ w_kv, 2, axis=-1)
    k = k.reshape(b, s, N_HEADS, 2, HEAD_DIM)
    v = v.reshape(b, s, N_HEADS, 2 * HEAD_DIM)
    return k, v


def diff_attention(h, k, v, w_q, lq1, lk1, lq2, lk2, g_subln, w_o, lambda_init):
    b, s, _ = h.shape
    q = (h @ w_q).reshape(b, s, N_HEADS, 2, HEAD_DIM) * (HEAD_DIM ** -0.5)
    f32 = jnp.float32
    lam = (jnp.exp(jnp.sum(lq1.astype(f32) * lk1.astype(f32)))
           - jnp.exp(jnp.sum(lq2.astype(f32) * lk2.astype(f32))) + lambda_init)
    n_blocks = s // Q_BLOCK
    q_blocks = jnp.moveaxis(q.reshape(b, n_blocks, Q_BLOCK, N_HEADS, 2, HEAD_DIM), 1, 0)
    key_pos = jnp.arange(s)

    def block(args):
        qb, i = args
        scores = jnp.einsum("bqhcd,bkhcd->bhcqk", qb, k).astype(f32)
        q_pos = i * Q_BLOCK + jnp.arange(Q_BLOCK)
        causal = key_pos[None, :] <= q_pos[:, None]
        p = jax.nn.softmax(jnp.where(causal, scores, -jnp.inf), axis=-1)
        a = p[:, :, 0] - lam * p[:, :, 1]
        return jnp.einsum("bhqk,bkhe->bqhe", a.astype(v.dtype), v)

    o = lax.map(block, (q_blocks, jnp.arange(n_blocks)))
    o = jnp.moveaxis(o, 0, 1).reshape(b, s, N_HEADS, 2 * HEAD_DIM)
    o = rms_norm(o, g_subln) * (1.0 - lambda_init)
    return o.reshape(b, s, D_MODEL) @ w_o


def setup_inputs(seed: int = 0) -> dict:
    key = jax.random.key(seed)
    ks = jax.random.split(key, 20)
    f32 = jnp.float32
    nrm = lambda k, shape, fan_in: jax.random.normal(k, shape, f32) * (fan_in ** -0.5)
    return {
        "x": jax.random.normal(ks[0], (BATCH, SEQ, D_MODEL), f32),
        "g_norm": 1.0 + 0.02 * jax.random.normal(ks[1], (DEPTH, 6, D_MODEL), f32),
        "w_ffn_gate": nrm(ks[2], (DEPTH, 2, D_MODEL, D_FF), D_MODEL),
        "w_ffn_up": nrm(ks[3], (DEPTH, 2, D_MODEL, D_FF), D_MODEL),
        "w_ffn_down": nrm(ks[4], (DEPTH, 2, D_FF, D_MODEL), D_FF),
        "w_conv_in": nrm(ks[5], (N_A_LAYERS, D_MODEL, 3 * D_MODEL), D_MODEL),
        "w_conv": nrm(ks[6], (N_A_LAYERS, CONV_WIDTH, D_MODEL), CONV_WIDTH),
        "w_conv_out": nrm(ks[7], (N_A_LAYERS, D_MODEL, D_MODEL), D_MODEL),
        "g_kv": 1.0 + 0.02 * jax.random.normal(ks[8], (D_MODEL,), f32),
        "w_kv": nrm(ks[9], (D_MODEL, 2 * D_MODEL), D_MODEL),
        "w_q": nrm(ks[10], (N_B_LAYERS, D_MODEL, D_MODEL), D_MODEL),
        "lambda_q1": LAMBDA_STD * jax.random.normal(ks[11], (N_B_LAYERS, HEAD_DIM), f32),
        "lambda_k1": LAMBDA_STD * jax.random.normal(ks[12], (N_B_LAYERS, HEAD_DIM), f32),
        "lambda_q2": LAMBDA_STD * jax.random.normal(ks[13], (N_B_LAYERS, HEAD_DIM), f32),
        "lambda_k2": LAMBDA_STD * jax.random.normal(ks[14], (N_B_LAYERS, HEAD_DIM), f32),
        "g_subln": 1.0 + 0.02 * jax.random.normal(ks[15], (N_B_LAYERS, 2 * HEAD_DIM), f32),
        "w_o": nrm(ks[16], (N_B_LAYERS, D_MODEL, D_MODEL), D_MODEL),
    }


def reference(x, g_norm, w_ffn_gate, w_ffn_up, w_ffn_down, w_conv_in, w_conv, w_conv_out,
              g_kv, w_kv, w_q, lambda_q1, lambda_k1, lambda_q2, lambda_k2, g_subln, w_o):
    k = v = None
    for layer in range(DEPTH):
        g = g_norm[layer]
        f = swiglu(rms_norm(x, g[0]), w_ffn_gate[layer, 0], w_ffn_up[layer, 0], w_ffn_down[layer, 0])
        x = x + 0.5 * rms_norm(f, g[1])
        h = rms_norm(x, g[2])
        if layer < N_A_LAYERS:
            m = short_conv_mixer(h, w_conv_in[layer], w_conv[layer], w_conv_out[layer])
        else:
            j = layer - N_A_LAYERS
            lambda_init = 0.8 - 0.6 * math.exp(-0.3 * layer)
            m = diff_attention(h, k, v, w_q[j], lambda_q1[j], lambda_k1[j], lambda_q2[j],
                               lambda_k2[j], g_subln[j], w_o[j], lambda_init)
        x = x + rms_norm(m, g[3])
        f = swiglu(rms_norm(x, g[4]), w_ffn_gate[layer, 1], w_ffn_up[layer, 1], w_ffn_down[layer, 1])
        x = x + 0.5 * rms_norm(f, g[5])
        if layer == N_A_LAYERS - 1:
            k, v = shared_kv(x, g_kv, w_kv)
    return x
```

```python
import functools
import math

import jax
import jax.numpy as jnp
from jax import lax
from jax.experimental import pallas as pl
from jax.experimental.pallas import tpu as pltpu

D_MODEL = 1024
DEPTH = 4
N_A_LAYERS = DEPTH // 2
D_FF = 2816
CONV_WIDTH = 3
HEAD_DIM = 64
HEAD_WIDTH = 2 * HEAD_DIM
N_HEADS = D_MODEL // HEAD_WIDTH
EPS = 1e-6

ROW_TILE = 512
FF_CHUNKS = ((0, 1536), (1536, 1280))
Q_TILE = 256
KV_TILE = 256
SUBLANES = 8
VMEM_LIMIT_BYTES = 56 * 1024 * 1024

_BF16 = jnp.bfloat16
_F32 = jnp.float32


def _rms(x, g):
    return x * lax.rsqrt(jnp.mean(x * x, axis=-1, keepdims=True) + EPS) * g


def _dot(a, b):
    return jnp.dot(a, b, preferred_element_type=_F32)


def _resident(shape, index_map):
    return pl.BlockSpec(shape, index_map, pipeline_mode=pl.Buffered(1))


def _params(semantics):
    return pltpu.CompilerParams(dimension_semantics=semantics,
                                vmem_limit_bytes=VMEM_LIMIT_BYTES)


def _ffn_kernel(x_ref, g_ref, wg_ref, wu_ref, wd_ref, o_ref):
    x = x_ref[...]
    h = _rms(x, g_ref[0:1, :]).astype(_BF16)
    f = None
    for start, width in FF_CHUNKS:
        gate = _dot(h, wg_ref[:, start:start + width])
        up = _dot(h, wu_ref[:, start:start + width])
        act = (gate * jax.nn.sigmoid(gate) * up).astype(_BF16)
        part = _dot(act, wd_ref[start:start + width, :])
        f = part if f is None else f + part
    o_ref[...] = x + 0.5 * _rms(f, g_ref[1:2, :])


def _ffn(x, g2, wg, wu, wd, layer, j):
    t = x.shape[0]
    w_in_spec = _resident((None, None, D_MODEL, D_FF), lambda i: (layer, j, 0, 0))
    w_out_spec = _resident((None, None, D_FF, D_MODEL), lambda i: (layer, j, 0, 0))
    row_spec = pl.BlockSpec((ROW_TILE, D_MODEL), lambda i: (i, 0))
    return pl.pallas_call(
        _ffn_kernel,
        out_shape=jax.ShapeDtypeStruct(x.shape, _F32),
        grid=(t // ROW_TILE,),
        in_specs=[row_spec, _resident((2, D_MODEL), lambda i: (0, 0)),
                  w_in_spec, w_in_spec, w_out_spec],
        out_specs=row_spec,
        compiler_params=_params(("parallel",)),
        name="ffn",
    )(x, g2, wg, wu, wd)


def _conv_kernel(tiles_per_seq, x_ref, g_ref, w_in_ref, w_conv_ref, w_out_ref, o_ref, tail_ref):
    x = x_ref[...]
    h = _rms(x, g_ref[0:1, :]).astype(_BF16)
    gate_b = _dot(h, w_in_ref[:, 0:D_MODEL])
    gate_c = _dot(h, w_in_ref[:, D_MODEL:2 * D_MODEL])
    u = _dot(h, w_in_ref[:, 2 * D_MODEL:3 * D_MODEL])
    cu = gate_c * u
    rows = cu.shape[0]

    seq_start = pl.program_id(0) % tiles_per_seq == 0
    tail = jnp.where(seq_start, 0.0, tail_ref[...])
    tail_ref[...] = cu[rows - SUBLANES:, :]

    row = lax.broadcasted_iota(jnp.int32, cu.shape, 0)
    prev1 = jnp.where(row == 0, tail[SUBLANES - 1:SUBLANES, :], pltpu.roll(cu, 1, axis=0))
    prev2 = jnp.where(row == 0, tail[SUBLANES - 2:SUBLANES - 1, :],
                      jnp.where(row == 1, tail[SUBLANES - 1:SUBLANES, :],
                                pltpu.roll(cu, 2, axis=0)))
    z = w_conv_ref[0:1, :] * prev2 + w_conv_ref[1:2, :] * prev1 + w_conv_ref[2:3, :] * cu
    m = _dot((gate_b * z).astype(_BF16), w_out_ref[...])
    o_ref[...] = x + _rms(m, g_ref[1:2, :])


def _conv_mixer(x, g2, w_in, w_conv, w_out, layer, seq):
    t = x.shape[0]
    row_spec = pl.BlockSpec((ROW_TILE, D_MODEL), lambda i: (i, 0))
    return pl.pallas_call(
        functools.partial(_conv_kernel, seq // ROW_TILE),
        out_shape=jax.ShapeDtypeStruct(x.shape, _F32),
        grid=(t // ROW_TILE,),
        in_specs=[row_spec, _resident((2, D_MODEL), lambda i: (0, 0)),
                  _resident((None, D_MODEL, 3 * D_MODEL), lambda i: (layer, 0, 0)),
                  _resident((None, CONV_WIDTH, D_MODEL), lambda i: (layer, 0, 0)),
                  _resident((None, D_MODEL, D_MODEL), lambda i: (layer, 0, 0))],
        out_specs=row_spec,
        scratch_shapes=[pltpu.VMEM((SUBLANES, D_MODEL), _F32)],
        compiler_params=_params(("arbitrary",)),
        name="conv_mixer",
    )(x, g2, w_in, w_conv, w_out)


def _kv_kernel(x_ref, g_ref, w_ref, k_ref, vt_ref):
    h = _rms(x_ref[...], g_ref[...]).astype(_BF16)
    k_ref[...] = _dot(h, w_ref[:, 0:D_MODEL]).astype(_BF16)
    v = _dot(h, w_ref[:, D_MODEL:2 * D_MODEL])
    for r in range(ROW_TILE // KV_TILE):
        vt_ref[r] = v[r * KV_TILE:(r + 1) * KV_TILE, :].T.astype(_BF16)


def _shared_kv(x, g_kv, w_kv):
    t = x.shape[0]
    per_step = ROW_TILE // KV_TILE
    return pl.pallas_call(
        _kv_kernel,
        out_shape=(jax.ShapeDtypeStruct((t, D_MODEL), _BF16),
                   jax.ShapeDtypeStruct((t // KV_TILE, D_MODEL, KV_TILE), _BF16)),
        grid=(t // ROW_TILE,),
        in_specs=[pl.BlockSpec((ROW_TILE, D_MODEL), lambda i: (i, 0)),
                  _resident((1, D_MODEL), lambda i: (0, 0)),
                  _resident((D_MODEL, 2 * D_MODEL), lambda i: (0, 0))],
        out_specs=(pl.BlockSpec((ROW_TILE, D_MODEL), lambda i: (i, 0)),
                   pl.BlockSpec((per_step, D_MODEL, KV_TILE), lambda i: (i, 0, 0))),
        compiler_params=_params(("parallel",)),
        name="shared_kv",
    )(x, g_kv, w_kv)


def _attn_kernel(lambda_init, x_ref, g_ref, wq_ref, lam_ref, gs_ref, wo_ref, k_ref, vt_ref,
                 o_ref, heads_ref):
    qi = pl.program_id(1)
    x = x_ref[...]
    h = _rms(x, g_ref[0:1, :]).astype(_BF16)
    q_t = (_dot(h, wq_ref[...]) * (HEAD_DIM ** -0.5)).T.astype(_BF16)

    lam_p = lam_ref[...]
    lam = (jnp.exp(jnp.sum(lam_p[0:1] * lam_p[1:2], axis=-1, keepdims=True))
           - jnp.exp(jnp.sum(lam_p[2:3] * lam_p[3:4], axis=-1, keepdims=True))
           + lambda_init)

    feat = lax.broadcasted_iota(jnp.int32, (HEAD_WIDTH, Q_TILE), 0)
    key_i = lax.broadcasted_iota(jnp.int32, (KV_TILE, 2 * Q_TILE), 0)
    qry_i = lax.broadcasted_iota(jnp.int32, (KV_TILE, 2 * Q_TILE), 1) % Q_TILE
    diag_ok = key_i <= qry_i

    for head in range(N_HEADS):
        cols = slice(head * HEAD_WIDTH, (head + 1) * HEAD_WIDTH)
        q_h = q_t[cols, :]
        q_both = jnp.concatenate([jnp.where(feat < HEAD_DIM, q_h, 0),
                                  jnp.where(feat >= HEAD_DIM, q_h, 0)], axis=1)

        def scores(j):
            k_blk = k_ref[pl.ds(pl.multiple_of(j * KV_TILE, KV_TILE), KV_TILE), cols]
            return _dot(k_blk, q_both)

        s = jnp.where(diag_ok, scores(qi), -jnp.inf)
        m0 = jnp.max(s, axis=0, keepdims=True)
        e = jnp.exp(s - m0)
        l0 = jnp.sum(e, axis=0, keepdims=True)
        acc0 = _dot(vt_ref[qi, cols, :], e.astype(_BF16))

        def step(j, carry):
            m, l, acc = carry
            s = scores(j)
            m_new = jnp.maximum(m, jnp.max(s, axis=0, keepdims=True))
            alpha = jnp.exp(m - m_new)
            e = jnp.exp(s - m_new)
            l = alpha * l + jnp.sum(e, axis=0, keepdims=True)
            acc = alpha * acc + _dot(vt_ref[j, cols, :], e.astype(_BF16))
            return m_new, l, acc

        _, l, acc = lax.fori_loop(0, qi, step, (m0, l0, acc0))
        p = acc / l
        o_h = p[:, :Q_TILE] - lam * p[:, Q_TILE:]
        o_h = (o_h * lax.rsqrt(jnp.mean(o_h * o_h, axis=0, keepdims=True) + EPS)
               * gs_ref[...] * (1.0 - lambda_init))
        heads_ref[:, cols] = o_h.T.astype(_BF16)

    mixed = _dot(heads_ref[...], wo_ref[...])
    o_ref[...] = x + _rms(mixed, g_ref[1:2, :])


def _diff_attention(x, g2, w_q, lam_params, g_subln_col, w_o, k, vt, layer, j, batch, seq):
    lambda_init = 0.8 - 0.6 * math.exp(-0.3 * layer)
    q_tiles = seq // Q_TILE
    kv_tiles = seq // KV_TILE
    row_spec = pl.BlockSpec((Q_TILE, D_MODEL), lambda b, i: (b * q_tiles + i, 0))
    return pl.pallas_call(
        functools.partial(_attn_kernel, lambda_init),
        out_shape=jax.ShapeDtypeStruct(x.shape, _F32),
        grid=(batch, q_tiles),
        in_specs=[row_spec,
                  _resident((2, D_MODEL), lambda b, i: (0, 0)),
                  _resident((None, D_MODEL, D_MODEL), lambda b, i: (j, 0, 0)),
                  _resident((None, 4, HEAD_DIM), lambda b, i: (j, 0, 0)),
                  _resident((None, HEAD_WIDTH, 1), lambda b, i: (j, 0, 0)),
                  _resident((None, D_MODEL, D_MODEL), lambda b, i: (j, 0, 0)),
                  pl.BlockSpec((seq, D_MODEL), lambda b, i: (b, 0)),
                  pl.BlockSpec((kv_tiles, D_MODEL, KV_TILE), lambda b, i: (b, 0, 0))],
        out_specs=row_spec,
        scratch_shapes=[pltpu.VMEM((Q_TILE, D_MODEL), _BF16)],
        compiler_params=_params(("parallel", "parallel")),
        name="diff_attention",
    )(x, g2, w_q, lam_params, g_subln_col, w_o, k, vt)


def kernel(x, g_norm, w_ffn_gate, w_ffn_up, w_ffn_down, w_conv_in, w_conv, w_conv_out, g_kv, w_kv,
           w_q, lambda_q1, lambda_k1, lambda_q2, lambda_k2, g_subln, w_o):
    batch, seq, d = x.shape
    assert d == D_MODEL and seq % ROW_TILE == 0 and seq % Q_TILE == 0 and Q_TILE == KV_TILE
    assert sum(w for _, w in FF_CHUNKS) == D_FF

    wg, wu, wd = (w.astype(_BF16) for w in (w_ffn_gate, w_ffn_up, w_ffn_down))
    w_conv_in, w_conv_out, w_kv, w_q, w_o = (
        w.astype(_BF16) for w in (w_conv_in, w_conv_out, w_kv, w_q, w_o))
    lam_params = jnp.stack([lambda_q1, lambda_k1, lambda_q2, lambda_k2], axis=1)
    g_subln_col = g_subln[:, :, None]

    x = x.reshape(batch * seq, d)
    k = vt = None
    for layer in range(DEPTH):
        g = g_norm[layer]
        x = _ffn(x, g[0:2], wg, wu, wd, layer, 0)
        if layer < N_A_LAYERS:
            x = _conv_mixer(x, g[2:4], w_conv_in, w_conv, w_conv_out, layer, seq)
        else:
            j = layer - N_A_LAYERS
            x = _diff_attention(x, g[2:4], w_q, lam_params, g_subln_col, w_o, k, vt,
                                layer, j, batch, seq)
        x = _ffn(x, g[4:6], wg, wu, wd, layer, 1)
        if layer == N_A_LAYERS - 1:
            k, vt = _shared_kv(x, g_kv.reshape(1, d), w_kv)
    return x.reshape(batch, seq, d)
```

```python
import functools
import math

import jax
import jax.numpy as jnp
from jax import lax
from jax.experimental import pallas as pl
from jax.experimental.pallas import tpu as pltpu

D_MODEL = 1024
DEPTH = 4
N_A_LAYERS = DEPTH // 2
D_FF = 2816
CONV_WIDTH = 3
HEAD_DIM = 64
HEAD_WIDTH = 2 * HEAD_DIM
N_HEADS = D_MODEL // HEAD_WIDTH
EPS = 1e-6

ROW_TILE = 512
FFN_ROW_TILE = 1024
FFN_SUB_TILE = 512
FF_CHUNKS = ((0, 1536), (1536, 1280))
Q_TILE = 256
KV_TILE = 256
SUBLANES = 8
BF16_SUBLANES = 16
V_ROWS = HEAD_WIDTH + BF16_SUBLANES
VMEM_LIMIT_BYTES = 56 * 1024 * 1024

_BF16 = jnp.bfloat16
_F32 = jnp.float32


def _rms(x, g):
    return x * lax.rsqrt(jnp.mean(x * x, axis=-1, keepdims=True) + EPS) * g


def _dot(a, b):
    return jnp.dot(a, b, preferred_element_type=_F32)


def _resident(shape, index_map):
    return pl.BlockSpec(shape, index_map, pipeline_mode=pl.Buffered(1))


def _params(semantics):
    return pltpu.CompilerParams(dimension_semantics=semantics,
                                vmem_limit_bytes=VMEM_LIMIT_BYTES)


def _ffn_kernel(x_ref, g_ref, wg_ref, wu_ref, wd_ref, o_ref):
    subs = [pl.ds(r * FFN_SUB_TILE, FFN_SUB_TILE) for r in range(FFN_ROW_TILE // FFN_SUB_TILE)]
    hs = [_rms(x_ref[rows, :], g_ref[0:1, :]).astype(_BF16) for rows in subs]
    for rows, h in zip(subs, hs):
        f = None
        for start, width in FF_CHUNKS:
            gate = _dot(h, wg_ref[:, start:start + width])
            up = _dot(h, wu_ref[:, start:start + width])
            act = (gate * jax.nn.sigmoid(gate) * up).astype(_BF16)
            part = _dot(act, wd_ref[start:start + width, :])
            f = part if f is None else f + part
        o_ref[rows, :] = x_ref[rows, :] + 0.5 * _rms(f, g_ref[1:2, :])


def _ffn(x, g2, wg, wu, wd, layer, j):
    t = x.shape[0]
    w_in_spec = _resident((None, None, D_MODEL, D_FF), lambda i: (layer, j, 0, 0))
    w_out_spec = _resident((None, None, D_FF, D_MODEL), lambda i: (layer, j, 0, 0))
    row_spec = pl.BlockSpec((FFN_ROW_TILE, D_MODEL), lambda i: (i, 0))
    return pl.pallas_call(
        _ffn_kernel,
        out_shape=jax.ShapeDtypeStruct(x.shape, _F32),
        grid=(t // FFN_ROW_TILE,),
        in_specs=[row_spec, _resident((2, D_MODEL), lambda i: (0, 0)),
                  w_in_spec, w_in_spec, w_out_spec],
        out_specs=row_spec,
        compiler_params=_params(("parallel",)),
        name="ffn",
    )(x, g2, wg, wu, wd)


def _conv_kernel(tiles_per_seq, x_ref, g_ref, w_in_ref, w_conv_ref, w_out_ref, o_ref, tail_ref):
    x = x_ref[...]
    h = _rms(x, g_ref[0:1, :]).astype(_BF16)
    gate_b = _dot(h, w_in_ref[:, 0:D_MODEL])
    gate_c = _dot(h, w_in_ref[:, D_MODEL:2 * D_MODEL])
    u = _dot(h, w_in_ref[:, 2 * D_MODEL:3 * D_MODEL])
    cu = gate_c * u
    rows = cu.shape[0]

    seq_start = pl.program_id(0) % tiles_per_seq == 0
    tail = jnp.where(seq_start, 0.0, tail_ref[...])
    tail_ref[...] = cu[rows - SUBLANES:, :]

    row = lax.broadcasted_iota(jnp.int32, cu.shape, 0)
    prev1 = jnp.where(row == 0, tail[SUBLANES - 1:SUBLANES, :], pltpu.roll(cu, 1, axis=0))
    prev2 = jnp.where(row == 0, tail[SUBLANES - 2:SUBLANES - 1, :],
                      jnp.where(row == 1, tail[SUBLANES - 1:SUBLANES, :],
                                pltpu.roll(cu, 2, axis=0)))
    z = w_conv_ref[0:1, :] * prev2 + w_conv_ref[1:2, :] * prev1 + w_conv_ref[2:3, :] * cu
    m = _dot((gate_b * z).astype(_BF16), w_out_ref[...])
    o_ref[...] = x + _rms(m, g_ref[1:2, :])


def _conv_mixer(x, g2, w_in, w_conv, w_out, layer, seq):
    t = x.shape[0]
    row_spec = pl.BlockSpec((ROW_TILE, D_MODEL), lambda i: (i, 0))
    return pl.pallas_call(
        functools.partial(_conv_kernel, seq // ROW_TILE),
        out_shape=jax.ShapeDtypeStruct(x.shape, _F32),
        grid=(t // ROW_TILE,),
        in_specs=[row_spec, _resident((2, D_MODEL), lambda i: (0, 0)),
                  _resident((None, D_MODEL, 3 * D_MODEL), lambda i: (layer, 0, 0)),
                  _resident((None, CONV_WIDTH, D_MODEL), lambda i: (layer, 0, 0)),
                  _resident((None, D_MODEL, D_MODEL), lambda i: (layer, 0, 0))],
        out_specs=row_spec,
        scratch_shapes=[pltpu.VMEM((SUBLANES, D_MODEL), _F32)],
        compiler_params=_params(("arbitrary",)),
        name="conv_mixer",
    )(x, g2, w_in, w_conv, w_out)


def _kv_kernel(x_ref, g_ref, w_ref, k_ref, vt_ref):
    h = _rms(x_ref[...], g_ref[...]).astype(_BF16)
    k_ref[...] = _dot(h, w_ref[:, 0:D_MODEL]).astype(_BF16)
    v = _dot(h, w_ref[:, D_MODEL:2 * D_MODEL])
    ones = jnp.ones((V_ROWS - HEAD_WIDTH, KV_TILE), _BF16)
    for r in range(ROW_TILE // KV_TILE):
        v_t = v[r * KV_TILE:(r + 1) * KV_TILE, :].T.astype(_BF16)
        for head in range(N_HEADS):
            vt_ref[r, head, 0:HEAD_WIDTH, :] = v_t[head * HEAD_WIDTH:(head + 1) * HEAD_WIDTH, :]
            vt_ref[r, head, HEAD_WIDTH:V_ROWS, :] = ones


def _shared_kv(x, g_kv, w_kv):
    t = x.shape[0]
    per_step = ROW_TILE // KV_TILE
    return pl.pallas_call(
        _kv_kernel,
        out_shape=(jax.ShapeDtypeStruct((t, D_MODEL), _BF16),
                   jax.ShapeDtypeStruct((t // KV_TILE, N_HEADS, V_ROWS, KV_TILE), _BF16)),
        grid=(t // ROW_TILE,),
        in_specs=[pl.BlockSpec((ROW_TILE, D_MODEL), lambda i: (i, 0)),
                  _resident((1, D_MODEL), lambda i: (0, 0)),
                  _resident((D_MODEL, 2 * D_MODEL), lambda i: (0, 0))],
        out_specs=(pl.BlockSpec((ROW_TILE, D_MODEL), lambda i: (i, 0)),
                   pl.BlockSpec((per_step, N_HEADS, V_ROWS, KV_TILE), lambda i: (i, 0, 0, 0))),
        compiler_params=_params(("parallel",)),
        name="shared_kv",
    )(x, g_kv, w_kv)


def _attn_kernel(lambda_init, x_ref, g_ref, wq_ref, lam_ref, gs_ref, wo_ref, k_ref, vt_ref,
                 o_ref, qb_ref, m_ref, acc_ref, heads_ref):
    qi = pl.program_id(1)
    x = x_ref[...]
    h = _rms(x, g_ref[0:1, :]).astype(_BF16)
    q_scale = HEAD_DIM ** -0.5 * math.log2(math.e)
    q_t = (_dot(h, wq_ref[...]) * q_scale).T.astype(_BF16)

    feat = lax.broadcasted_iota(jnp.int32, (HEAD_WIDTH, Q_TILE), 0)
    zero = jnp.zeros((), _BF16)
    for head in range(N_HEADS):
        q_h = q_t[head * HEAD_WIDTH:(head + 1) * HEAD_WIDTH, :]
        qb_ref[head, :, 0:Q_TILE] = jnp.where(feat < HEAD_DIM, q_h, zero)
        qb_ref[head, :, Q_TILE:2 * Q_TILE] = jnp.where(feat >= HEAD_DIM, q_h, zero)

    def kv_block(j, diagonal):
        k_rows = pl.ds(pl.multiple_of(j * KV_TILE, KV_TILE), KV_TILE)
        if diagonal:
            key_i = lax.broadcasted_iota(jnp.int32, (KV_TILE, 2 * Q_TILE), 0)
            qry_i = lax.broadcasted_iota(jnp.int32, (KV_TILE, 2 * Q_TILE), 1) % Q_TILE
            visible = key_i <= qry_i

        def scores(head):
            k_blk = k_ref[k_rows, head * HEAD_WIDTH:(head + 1) * HEAD_WIDTH]
            return _dot(k_blk, qb_ref[head])

        s_next = scores(0)
        for head in range(N_HEADS):
            s = s_next
            if head + 1 < N_HEADS:
                s_next = scores(head + 1)
            if diagonal:
                s = jnp.where(visible, s, -jnp.inf)
                m_new = jnp.max(s, axis=0, keepdims=True)
            else:
                m_old = m_ref[head]
                m_new = jnp.maximum(m_old, jnp.max(s, axis=0, keepdims=True))
            e = jnp.exp2(s - m_new).astype(_BF16)
            pv = _dot(vt_ref[j, head], e)
            if diagonal:
                acc_ref[head] = pv
            else:
                acc_ref[head] = jnp.exp2(m_old - m_new) * acc_ref[head] + pv
            m_ref[head] = m_new

    kv_block(qi, diagonal=True)

    def off_diagonal(j, carry):
        kv_block(j, diagonal=False)
        return carry

    lax.fori_loop(0, qi, off_diagonal, 0)

    lam_p = lam_ref[...]
    lam = (jnp.exp(jnp.sum(lam_p[0:1] * lam_p[1:2], axis=-1, keepdims=True))
           - jnp.exp(jnp.sum(lam_p[2:3] * lam_p[3:4], axis=-1, keepdims=True))
           + lambda_init)
    for head in range(N_HEADS):
        acc = acc_ref[head]
        p = acc[0:HEAD_WIDTH, :] * (1.0 / acc[HEAD_WIDTH:HEAD_WIDTH + 1, :])
        o_h = p[:, :Q_TILE] - lam * p[:, Q_TILE:]
        o_h = (o_h * lax.rsqrt(jnp.mean(o_h * o_h, axis=0, keepdims=True) + EPS)
               * gs_ref[...] * (1.0 - lambda_init))
        heads_ref[:, head * HEAD_WIDTH:(head + 1) * HEAD_WIDTH] = o_h.T.astype(_BF16)

    mixed = _dot(heads_ref[...], wo_ref[...])
    o_ref[...] = x + _rms(mixed, g_ref[1:2, :])


def _diff_attention(x, g2, w_q, lam_params, g_subln_col, w_o, k, vt, layer, j, batch, seq):
    lambda_init = 0.8 - 0.6 * math.exp(-0.3 * layer)
    q_tiles = seq // Q_TILE
    kv_tiles = seq // KV_TILE
    row_spec = pl.BlockSpec((Q_TILE, D_MODEL), lambda b, i: (b * q_tiles + i, 0))
    return pl.pallas_call(
        functools.partial(_attn_kernel, lambda_init),
        out_shape=jax.ShapeDtypeStruct(x.shape, _F32),
        grid=(batch, q_tiles),
        in_specs=[row_spec,
                  _resident((2, D_MODEL), lambda b, i: (0, 0)),
                  _resident((None, D_MODEL, D_MODEL), lambda b, i: (j, 0, 0)),
                  _resident((None, 4, HEAD_DIM), lambda b, i: (j, 0, 0)),
                  _resident((None, HEAD_WIDTH, 1), lambda b, i: (j, 0, 0)),
                  _resident((None, D_MODEL, D_MODEL), lambda b, i: (j, 0, 0)),
                  pl.BlockSpec((seq, D_MODEL), lambda b, i: (b, 0)),
                  pl.BlockSpec((kv_tiles, N_HEADS, V_ROWS, KV_TILE), lambda b, i: (b, 0, 0, 0))],
        out_specs=row_spec,
        scratch_shapes=[pltpu.VMEM((N_HEADS, HEAD_WIDTH, 2 * Q_TILE), _BF16),
                        pltpu.VMEM((N_HEADS, 1, 2 * Q_TILE), _F32),
                        pltpu.VMEM((N_HEADS, V_ROWS, 2 * Q_TILE), _F32),
                        pltpu.VMEM((Q_TILE, D_MODEL), _BF16)],
        compiler_params=_params(("parallel", "parallel")),
        name="diff_attention",
    )(x, g2, w_q, lam_params, g_subln_col, w_o, k, vt)


def kernel(x, g_norm, w_ffn_gate, w_ffn_up, w_ffn_down, w_conv_in, w_conv, w_conv_out, g_kv, w_kv,
           w_q, lambda_q1, lambda_k1, lambda_q2, lambda_k2, g_subln, w_o):
    batch, seq, d = x.shape
    assert d == D_MODEL and seq % ROW_TILE == 0 and seq % Q_TILE == 0 and Q_TILE == KV_TILE
    assert sum(w for _, w in FF_CHUNKS) == D_FF

    wg, wu, wd = (w.astype(_BF16) for w in (w_ffn_gate, w_ffn_up, w_ffn_down))
    w_conv_in, w_conv_out, w_kv, w_q, w_o = (
        w.astype(_BF16) for w in (w_conv_in, w_conv_out, w_kv, w_q, w_o))
    lam_params = jnp.stack([lambda_q1, lambda_k1, lambda_q2, lambda_k2], axis=1)
    g_subln_col = g_subln[:, :, None]

    x = x.reshape(batch * seq, d)
    k = vt = None
    for layer in range(DEPTH):
        g = g_norm[layer]
        x = _ffn(x, g[0:2], wg, wu, wd, layer, 0)
        if layer < N_A_LAYERS:
            x = _conv_mixer(x, g[2:4], w_conv_in, w_conv, w_conv_out, layer, seq)
        else:
            j = layer - N_A_LAYERS
            x = _diff_attention(x, g[2:4], w_q, lam_params, g_subln_col, w_o, k, vt,
                                layer, j, batch, seq)
        x = _ffn(x, g[4:6], wg, wu, wd, layer, 1)
        if layer == N_A_LAYERS - 1:
            k, vt = _shared_kv(x, g_kv.reshape(1, d), w_kv)
    return x.reshape(batch, seq, d)
```

```python
import functools
import math

import jax
import jax.numpy as jnp
from jax import lax
from jax.experimental import pallas as pl
from jax.experimental.pallas import tpu as pltpu

D_MODEL = 1024
DEPTH = 4
N_A_LAYERS = DEPTH // 2
D_FF = 2816
CONV_WIDTH = 3
HEAD_DIM = 64
HEAD_WIDTH = 2 * HEAD_DIM
N_HEADS = D_MODEL // HEAD_WIDTH
EPS = 1e-6

ROW_TILE = 512
FFN_ROW_TILE = 1024
FFN_SUB_TILE = 512
CONV_ROW_TILE = 1024
CONV_SUB_TILE = 512
FF_CHUNKS = ((0, 1536), (1536, 1280))
Q_TILE = 256
KV_TILE = 256
SUBLANES = 8
BF16_SUBLANES = 16
V_ROWS = HEAD_WIDTH + BF16_SUBLANES
VMEM_LIMIT_BYTES = 56 * 1024 * 1024

_BF16 = jnp.bfloat16
_F32 = jnp.float32


def _rms(x, g):
    return x * lax.rsqrt(jnp.mean(x * x, axis=-1, keepdims=True) + EPS) * g


def _dot(a, b):
    return jnp.dot(a, b, preferred_element_type=_F32)


def _resident(shape, index_map):
    return pl.BlockSpec(shape, index_map, pipeline_mode=pl.Buffered(1))


def _params(semantics):
    return pltpu.CompilerParams(dimension_semantics=semantics,
                                vmem_limit_bytes=VMEM_LIMIT_BYTES)


def _ffn_kernel(x_ref, g_ref, wg_ref, wu_ref, wd_ref, o_ref):
    subs = [pl.ds(r * FFN_SUB_TILE, FFN_SUB_TILE) for r in range(FFN_ROW_TILE // FFN_SUB_TILE)]
    hs = [_rms(x_ref[rows, :], g_ref[0:1, :]).astype(_BF16) for rows in subs]
    for rows, h in zip(subs, hs):
        f = None
        for start, width in FF_CHUNKS:
            gate = _dot(h, wg_ref[:, start:start + width])
            up = _dot(h, wu_ref[:, start:start + width])
            act = (gate * jax.nn.sigmoid(gate) * up).astype(_BF16)
            part = _dot(act, wd_ref[start:start + width, :])
            f = part if f is None else f + part
        o_ref[rows, :] = x_ref[rows, :] + 0.5 * _rms(f, g_ref[1:2, :])


def _ffn(x, g2, wg, wu, wd, layer, j):
    t = x.shape[0]
    w_in_spec = _resident((None, None, D_MODEL, D_FF), lambda i: (layer, j, 0, 0))
    w_out_spec = _resident((None, None, D_FF, D_MODEL), lambda i: (layer, j, 0, 0))
    row_spec = pl.BlockSpec((FFN_ROW_TILE, D_MODEL), lambda i: (i, 0))
    return pl.pallas_call(
        _ffn_kernel,
        out_shape=jax.ShapeDtypeStruct(x.shape, _F32),
        grid=(t // FFN_ROW_TILE,),
        in_specs=[row_spec, _resident((2, D_MODEL), lambda i: (0, 0)),
                  w_in_spec, w_in_spec, w_out_spec],
        out_specs=row_spec,
        compiler_params=_params(("parallel",)),
        name="ffn",
    )(x, g2, wg, wu, wd)


def _conv_kernel(tiles_per_seq, x_ref, g_ref, w_in_ref, w_conv_ref, w_out_ref, o_ref, tail_ref):
    seq_start = pl.program_id(0) % tiles_per_seq == 0
    tail = jnp.where(seq_start, 0.0, tail_ref[...])
    row = lax.broadcasted_iota(jnp.int32, (CONV_SUB_TILE, D_MODEL), 0)
    for r in range(CONV_ROW_TILE // CONV_SUB_TILE):
        rows = pl.ds(r * CONV_SUB_TILE, CONV_SUB_TILE)
        x = x_ref[rows, :]
        h = _rms(x, g_ref[0:1, :]).astype(_BF16)
        gate_b = _dot(h, w_in_ref[:, 0:D_MODEL])
        gate_c = _dot(h, w_in_ref[:, D_MODEL:2 * D_MODEL])
        u = _dot(h, w_in_ref[:, 2 * D_MODEL:3 * D_MODEL])
        cu = gate_c * u
        prev1 = jnp.where(row == 0, tail[SUBLANES - 1:SUBLANES, :], pltpu.roll(cu, 1, axis=0))
        prev2 = jnp.where(row == 0, tail[SUBLANES - 2:SUBLANES - 1, :],
                          jnp.where(row == 1, tail[SUBLANES - 1:SUBLANES, :],
                                    pltpu.roll(cu, 2, axis=0)))
        tail = cu[CONV_SUB_TILE - SUBLANES:, :]
        z = w_conv_ref[0:1, :] * prev2 + w_conv_ref[1:2, :] * prev1 + w_conv_ref[2:3, :] * cu
        m = _dot((gate_b * z).astype(_BF16), w_out_ref[...])
        o_ref[rows, :] = x + _rms(m, g_ref[1:2, :])
    tail_ref[...] = tail


def _conv_mixer(x, g2, w_in, w_conv, w_out, layer, seq):
    t = x.shape[0]
    row_spec = pl.BlockSpec((CONV_ROW_TILE, D_MODEL), lambda i: (i, 0))
    return pl.pallas_call(
        functools.partial(_conv_kernel, seq // CONV_ROW_TILE),
        out_shape=jax.ShapeDtypeStruct(x.shape, _F32),
        grid=(t // CONV_ROW_TILE,),
        in_specs=[row_spec, _resident((2, D_MODEL), lambda i: (0, 0)),
                  _resident((None, D_MODEL, 3 * D_MODEL), lambda i: (layer, 0, 0)),
                  _resident((None, CONV_WIDTH, D_MODEL), lambda i: (layer, 0, 0)),
                  _resident((None, D_MODEL, D_MODEL), lambda i: (layer, 0, 0))],
        out_specs=row_spec,
        scratch_shapes=[pltpu.VMEM((SUBLANES, D_MODEL), _F32)],
        compiler_params=_params(("arbitrary",)),
        name="conv_mixer",
    )(x, g2, w_in, w_conv, w_out)


def _kv_kernel(x_ref, g_ref, w_ref, k_ref, vt_ref):
    h = _rms(x_ref[...], g_ref[...]).astype(_BF16)
    k_ref[...] = _dot(h, w_ref[:, 0:D_MODEL]).astype(_BF16)
    v = _dot(h, w_ref[:, D_MODEL:2 * D_MODEL])
    ones = jnp.ones((V_ROWS - HEAD_WIDTH, KV_TILE), _BF16)
    for r in range(ROW_TILE // KV_TILE):
        v_t = v[r * KV_TILE:(r + 1) * KV_TILE, :].T.astype(_BF16)
        for head in range(N_HEADS):
            vt_ref[r, head, 0:HEAD_WIDTH, :] = v_t[head * HEAD_WIDTH:(head + 1) * HEAD_WIDTH, :]
            vt_ref[r, head, HEAD_WIDTH:V_ROWS, :] = ones


def _shared_kv(x, g_kv, w_kv):
    t = x.shape[0]
    per_step = ROW_TILE // KV_TILE
    return pl.pallas_call(
        _kv_kernel,
        out_shape=(jax.ShapeDtypeStruct((t, D_MODEL), _BF16),
                   jax.ShapeDtypeStruct((t // KV_TILE, N_HEADS, V_ROWS, KV_TILE), _BF16)),
        grid=(t // ROW_TILE,),
        in_specs=[pl.BlockSpec((ROW_TILE, D_MODEL), lambda i: (i, 0)),
                  _resident((1, D_MODEL), lambda i: (0, 0)),
                  _resident((D_MODEL, 2 * D_MODEL), lambda i: (0, 0))],
        out_specs=(pl.BlockSpec((ROW_TILE, D_MODEL), lambda i: (i, 0)),
                   pl.BlockSpec((per_step, N_HEADS, V_ROWS, KV_TILE), lambda i: (i, 0, 0, 0))),
        compiler_params=_params(("parallel",)),
        name="shared_kv",
    )(x, g_kv, w_kv)


def _attn_kernel(lambda_init, x_ref, g_ref, wq_ref, lam_ref, gs_ref, wo_ref, k_ref, vt_ref,
                 o_ref, qb_ref, m_ref, acc_ref, heads_ref):
    qi = pl.program_id(1)
    x = x_ref[...]
    h = _rms(x, g_ref[0:1, :]).astype(_BF16)
    q_scale = HEAD_DIM ** -0.5 * math.log2(math.e)
    q_t = (_dot(h, wq_ref[...]) * q_scale).T.astype(_BF16)

    feat = lax.broadcasted_iota(jnp.int32, (HEAD_WIDTH, Q_TILE), 0)
    zero = jnp.zeros((), _BF16)
    for head in range(N_HEADS):
        q_h = q_t[head * HEAD_WIDTH:(head + 1) * HEAD_WIDTH, :]
        qb_ref[head, :, 0:Q_TILE] = jnp.where(feat < HEAD_DIM, q_h, zero)
        qb_ref[head, :, Q_TILE:2 * Q_TILE] = jnp.where(feat >= HEAD_DIM, q_h, zero)

    def kv_block(j, diagonal):
        k_rows = pl.ds(pl.multiple_of(j * KV_TILE, KV_TILE), KV_TILE)
        if diagonal:
            key_i = lax.broadcasted_iota(jnp.int32, (KV_TILE, 2 * Q_TILE), 0)
            qry_i = lax.broadcasted_iota(jnp.int32, (KV_TILE, 2 * Q_TILE), 1) % Q_TILE
            visible = key_i <= qry_i

        def scores(head):
            k_blk = k_ref[k_rows, head * HEAD_WIDTH:(head + 1) * HEAD_WIDTH]
            return _dot(k_blk, qb_ref[head])

        def fold(head, e, alpha):
            pv = _dot(vt_ref[j, head], e)
            acc_ref[head] = pv if diagonal else alpha * acc_ref[head] + pv

        s_next = scores(0)
        pending = None
        for head in range(N_HEADS):
            s = s_next
            if head + 1 < N_HEADS:
                s_next = scores(head + 1)
            alpha = None
            if diagonal:
                s = jnp.where(visible, s, -jnp.inf)
                m_new = jnp.max(s, axis=0, keepdims=True)
            else:
                m_old = m_ref[head]
                m_new = jnp.maximum(m_old, jnp.max(s, axis=0, keepdims=True))
                alpha = jnp.exp2(m_old - m_new)
            m_ref[head] = m_new
            e = jnp.exp2(s - m_new).astype(_BF16)
            if pending is not None:
                fold(*pending)
            pending = (head, e, alpha)
        fold(*pending)

    kv_block(qi, diagonal=True)

    def off_diagonal(j, carry):
        kv_block(j, diagonal=False)
        return carry

    lax.fori_loop(0, qi, off_diagonal, 0)

    lam_p = lam_ref[...]
    lam = (jnp.exp(jnp.sum(lam_p[0:1] * lam_p[1:2], axis=-1, keepdims=True))
           - jnp.exp(jnp.sum(lam_p[2:3] * lam_p[3:4], axis=-1, keepdims=True))
           + lambda_init)
    for head in range(N_HEADS):
        acc = acc_ref[head]
        p = acc[0:HEAD_WIDTH, :] * (1.0 / acc[HEAD_WIDTH:HEAD_WIDTH + 1, :])
        o_h = p[:, :Q_TILE] - lam * p[:, Q_TILE:]
        o_h = (o_h * lax.rsqrt(jnp.mean(o_h * o_h, axis=0, keepdims=True) + EPS)
               * gs_ref[...] * (1.0 - lambda_init))
        heads_ref[:, head * HEAD_WIDTH:(head + 1) * HEAD_WIDTH] = o_h.T.astype(_BF16)

    mixed = _dot(heads_ref[...], wo_ref[...])
    o_ref[...] = x + _rms(mixed, g_ref[1:2, :])


def _diff_attention(x, g2, w_q, lam_params, g_subln_col, w_o, k, vt, layer, j, batch, seq):
    lambda_init = 0.8 - 0.6 * math.exp(-0.3 * layer)
    q_tiles = seq // Q_TILE
    kv_tiles = seq // KV_TILE
    row_spec = pl.BlockSpec((Q_TILE, D_MODEL), lambda b, i: (b * q_tiles + i, 0))
    return pl.pallas_call(
        functools.partial(_attn_kernel, lambda_init),
        out_shape=jax.ShapeDtypeStruct(x.shape, _F32),
        grid=(batch, q_tiles),
        in_specs=[row_spec,
                  _resident((2, D_MODEL), lambda b, i: (0, 0)),
                  _resident((None, D_MODEL, D_MODEL), lambda b, i: (j, 0, 0)),
                  _resident((None, 4, HEAD_DIM), lambda b, i: (j, 0, 0)),
                  _resident((None, HEAD_WIDTH, 1), lambda b, i: (j, 0, 0)),
                  _resident((None, D_MODEL, D_MODEL), lambda b, i: (j, 0, 0)),
                  pl.BlockSpec((seq, D_MODEL), lambda b, i: (b, 0)),
                  pl.BlockSpec((kv_tiles, N_HEADS, V_ROWS, KV_TILE), lambda b, i: (b, 0, 0, 0))],
        out_specs=row_spec,
        scratch_shapes=[pltpu.VMEM((N_HEADS, HEAD_WIDTH, 2 * Q_TILE), _BF16),
                        pltpu.VMEM((N_HEADS, 1, 2 * Q_TILE), _F32),
                        pltpu.VMEM((N_HEADS, V_ROWS, 2 * Q_TILE), _F32),
                        pltpu.VMEM((Q_TILE, D_MODEL), _BF16)],
        compiler_params=_params(("parallel", "parallel")),
        name="diff_attention",
    )(x, g2, w_q, lam_params, g_subln_col, w_o, k, vt)


def kernel(x, g_norm, w_ffn_gate, w_ffn_up, w_ffn_down, w_conv_in, w_conv, w_conv_out, g_kv, w_kv,
           w_q, lambda_q1, lambda_k1, lambda_q2, lambda_k2, g_subln, w_o):
    batch, seq, d = x.shape
    assert d == D_MODEL and seq % Q_TILE == 0 and Q_TILE == KV_TILE
    assert seq % ROW_TILE == 0 and seq % CONV_ROW_TILE == 0 and (batch * seq) % FFN_ROW_TILE == 0
    assert sum(w for _, w in FF_CHUNKS) == D_FF

    wg, wu, wd = (w.astype(_BF16) for w in (w_ffn_gate, w_ffn_up, w_ffn_down))
    w_conv_in, w_conv_out, w_kv, w_q, w_o = (
        w.astype(_BF16) for w in (w_conv_in, w_conv_out, w_kv, w_q, w_o))
    lam_params = jnp.stack([lambda_q1, lambda_k1, lambda_q2, lambda_k2], axis=1)
    g_subln_col = g_subln[:, :, None]

    x = x.reshape(batch * seq, d)
    k = vt = None
    for layer in range(DEPTH):
        g = g_norm[layer]
        x = _ffn(x, g[0:2], wg, wu, wd, layer, 0)
        if layer < N_A_LAYERS:
            x = _conv_mixer(x, g[2:4], w_conv_in, w_conv, w_conv_out, layer, seq)
        else:
            j = layer - N_A_LAYERS
            x = _diff_attention(x, g[2:4], w_q, lam_params, g_subln_col, w_o, k, vt,
                                layer, j, batch, seq)
        x = _ffn(x, g[4:6], wg, wu, wd, layer, 1)
        if layer == N_A_LAYERS - 1:
            k, vt = _shared_kv(x, g_kv.reshape(1, d), w_kv)
    return x.reshape(batch, seq, d)
```

```python
import functools
import math

import jax
import jax.numpy as jnp
from jax import lax
from jax.experimental import pallas as pl
from jax.experimental.pallas import tpu as pltpu

D_MODEL = 1024
DEPTH = 4
N_A_LAYERS = DEPTH // 2
D_FF = 2816
CONV_WIDTH = 3
HEAD_DIM = 64
HEAD_WIDTH = 2 * HEAD_DIM
N_HEADS = D_MODEL // HEAD_WIDTH
EPS = 1e-6

ROW_TILE = 512
FFN_ROW_TILE = 1024
FFN_SUB_TILE = 512
CONV_ROW_TILE = 1024
CONV_SUB_TILE = 512
FF_CHUNKS = ((0, 1536), (1536, 1280))
Q_TILE = 256
KV_TILE = 256
QK_LOOKAHEAD = 1
SUBLANES = 8
BF16_SUBLANES = 16
V_ROWS = HEAD_WIDTH + BF16_SUBLANES
VMEM_LIMIT_BYTES = 56 * 1024 * 1024

_BF16 = jnp.bfloat16
_F32 = jnp.float32


def _rms(x, g):
    return x * lax.rsqrt(jnp.mean(x * x, axis=-1, keepdims=True) + EPS) * g


def _dot(a, b):
    return jnp.dot(a, b, preferred_element_type=_F32)


def _resident(shape, index_map):
    return pl.BlockSpec(shape, index_map, pipeline_mode=pl.Buffered(1))


def _params(semantics):
    return pltpu.CompilerParams(dimension_semantics=semantics,
                                vmem_limit_bytes=VMEM_LIMIT_BYTES)


def _ffn_kernel(n_casts, x_ref, g_ref, wg_ref, wu_ref, wd_ref, *refs):
    src_refs, o_ref, dst_refs = refs[:n_casts], refs[n_casts], refs[n_casts + 1:]
    subs = [pl.ds(r * FFN_SUB_TILE, FFN_SUB_TILE) for r in range(FFN_ROW_TILE // FFN_SUB_TILE)]
    hs = [_rms(x_ref[rows, :], g_ref[0:1, :]).astype(_BF16) for rows in subs]
    for rows, h in zip(subs, hs):
        f = None
        for start, width in FF_CHUNKS:
            gate = _dot(h, wg_ref[:, start:start + width])
            up = _dot(h, wu_ref[:, start:start + width])
            act = (gate * jax.nn.sigmoid(gate) * up).astype(_BF16)
            part = _dot(act, wd_ref[start:start + width, :])
            f = part if f is None else f + part
        o_ref[rows, :] = x_ref[rows, :] + 0.5 * _rms(f, g_ref[1:2, :])
    for src_ref, dst_ref in zip(src_refs, dst_refs):
        dst_ref[...] = src_ref[...].astype(_BF16)


def _ffn(x, g2, wg, wu, wd, casts=()):
    t = x.shape[0]
    steps = t // FFN_ROW_TILE
    row_spec = pl.BlockSpec((FFN_ROW_TILE, D_MODEL), lambda i: (i, 0))
    cast_in_specs, cast_out_specs, cast_shapes = [], [], []
    for w, lead in casts:
        rows, cols = w.shape[len(lead):]
        assert rows % (steps * BF16_SUBLANES) == 0
        cast_in_specs.append(pl.BlockSpec((None,) * len(lead) + (rows // steps, cols),
                                          lambda i, lead=lead: lead + (i, 0)))
        cast_out_specs.append(pl.BlockSpec((rows // steps, cols), lambda i: (i, 0)))
        cast_shapes.append(jax.ShapeDtypeStruct((rows, cols), _BF16))
    out, *converted = pl.pallas_call(
        functools.partial(_ffn_kernel, len(casts)),
        out_shape=[jax.ShapeDtypeStruct(x.shape, _F32)] + cast_shapes,
        grid=(steps,),
        in_specs=[row_spec, _resident((2, D_MODEL), lambda i: (0, 0)),
                  _resident((D_MODEL, D_FF), lambda i: (0, 0)),
                  _resident((D_MODEL, D_FF), lambda i: (0, 0)),
                  _resident((D_FF, D_MODEL), lambda i: (0, 0))] + cast_in_specs,
        out_specs=[row_spec] + cast_out_specs,
        compiler_params=_params(("parallel",)),
        name="ffn",
    )(x, g2, wg, wu, wd, *(w for w, _ in casts))
    return out, converted


def _conv_kernel(tiles_per_seq, x_ref, g_ref, w_in_ref, w_conv_ref, w_out_ref, o_ref, tail_ref):
    seq_start = pl.program_id(0) % tiles_per_seq == 0
    tail = jnp.where(seq_start, 0.0, tail_ref[...])
    row = lax.broadcasted_iota(jnp.int32, (CONV_SUB_TILE, D_MODEL), 0)
    for r in range(CONV_ROW_TILE // CONV_SUB_TILE):
        rows = pl.ds(r * CONV_SUB_TILE, CONV_SUB_TILE)
        x = x_ref[rows, :]
        h = _rms(x, g_ref[0:1, :]).astype(_BF16)
        gate_b = _dot(h, w_in_ref[:, 0:D_MODEL])
        gate_c = _dot(h, w_in_ref[:, D_MODEL:2 * D_MODEL])
        u = _dot(h, w_in_ref[:, 2 * D_MODEL:3 * D_MODEL])
        cu = gate_c * u
        prev1 = jnp.where(row == 0, tail[SUBLANES - 1:SUBLANES, :], pltpu.roll(cu, 1, axis=0))
        prev2 = jnp.where(row == 0, tail[SUBLANES - 2:SUBLANES - 1, :],
                          jnp.where(row == 1, tail[SUBLANES - 1:SUBLANES, :],
                                    pltpu.roll(cu, 2, axis=0)))
        tail = cu[CONV_SUB_TILE - SUBLANES:, :]
        z = w_conv_ref[0:1, :] * prev2 + w_conv_ref[1:2, :] * prev1 + w_conv_ref[2:3, :] * cu
        m = _dot((gate_b * z).astype(_BF16), w_out_ref[...])
        o_ref[rows, :] = x + _rms(m, g_ref[1:2, :])
    tail_ref[...] = tail


def _conv_mixer(x, g2, w_in, w_conv, w_out, layer, seq):
    t = x.shape[0]
    row_spec = pl.BlockSpec((CONV_ROW_TILE, D_MODEL), lambda i: (i, 0))
    return pl.pallas_call(
        functools.partial(_conv_kernel, seq // CONV_ROW_TILE),
        out_shape=jax.ShapeDtypeStruct(x.shape, _F32),
        grid=(t // CONV_ROW_TILE,),
        in_specs=[row_spec, _resident((2, D_MODEL), lambda i: (0, 0)),
                  _resident((D_MODEL, 3 * D_MODEL), lambda i: (0, 0)),
                  _resident((None, CONV_WIDTH, D_MODEL), lambda i: (layer, 0, 0)),
                  _resident((D_MODEL, D_MODEL), lambda i: (0, 0))],
        out_specs=row_spec,
        scratch_shapes=[pltpu.VMEM((SUBLANES, D_MODEL), _F32)],
        compiler_params=_params(("arbitrary",)),
        name="conv_mixer",
    )(x, g2, w_in, w_conv, w_out)


def _kv_kernel(x_ref, g_ref, w_ref, k_ref, vt_ref):
    h = _rms(x_ref[...], g_ref[...]).astype(_BF16)
    k_ref[...] = _dot(h, w_ref[:, 0:D_MODEL]).astype(_BF16)
    v = _dot(h, w_ref[:, D_MODEL:2 * D_MODEL])
    ones = jnp.ones((V_ROWS - HEAD_WIDTH, KV_TILE), _BF16)
    for r in range(ROW_TILE // KV_TILE):
        v_t = v[r * KV_TILE:(r + 1) * KV_TILE, :].T.astype(_BF16)
        for head in range(N_HEADS):
            vt_ref[r, head, 0:HEAD_WIDTH, :] = v_t[head * HEAD_WIDTH:(head + 1) * HEAD_WIDTH, :]
            vt_ref[r, head, HEAD_WIDTH:V_ROWS, :] = ones


def _shared_kv(x, g_kv, w_kv):
    t = x.shape[0]
    per_step = ROW_TILE // KV_TILE
    return pl.pallas_call(
        _kv_kernel,
        out_shape=(jax.ShapeDtypeStruct((t, D_MODEL), _BF16),
                   jax.ShapeDtypeStruct((t // KV_TILE, N_HEADS, V_ROWS, KV_TILE), _BF16)),
        grid=(t // ROW_TILE,),
        in_specs=[pl.BlockSpec((ROW_TILE, D_MODEL), lambda i: (i, 0)),
                  _resident((1, D_MODEL), lambda i: (0, 0)),
                  _resident((D_MODEL, 2 * D_MODEL), lambda i: (0, 0))],
        out_specs=(pl.BlockSpec((ROW_TILE, D_MODEL), lambda i: (i, 0)),
                   pl.BlockSpec((per_step, N_HEADS, V_ROWS, KV_TILE), lambda i: (i, 0, 0, 0))),
        compiler_params=_params(("parallel",)),
        name="shared_kv",
    )(x, g_kv, w_kv)


def _attn_kernel(lambda_init, x_ref, g_ref, wq_ref, lam_ref, gs_ref, wo_ref, k_ref, vt_ref,
                 o_ref, qb_ref, m_ref, acc_ref, heads_ref):
    qi = pl.program_id(1)
    x = x_ref[...]
    h = _rms(x, g_ref[0:1, :]).astype(_BF16)
    q_scale = HEAD_DIM ** -0.5 * math.log2(math.e)
    q_t = (_dot(h, wq_ref[...]) * q_scale).T.astype(_BF16)

    feat = lax.broadcasted_iota(jnp.int32, (HEAD_WIDTH, Q_TILE), 0)
    zero = jnp.zeros((), _BF16)
    for head in range(N_HEADS):
        q_h = q_t[head * HEAD_WIDTH:(head + 1) * HEAD_WIDTH, :]
        qb_ref[head, :, 0:Q_TILE] = jnp.where(feat < HEAD_DIM, q_h, zero)
        qb_ref[head, :, Q_TILE:2 * Q_TILE] = jnp.where(feat >= HEAD_DIM, q_h, zero)

    def kv_block(j, diagonal):
        k_rows = pl.ds(pl.multiple_of(j * KV_TILE, KV_TILE), KV_TILE)
        if diagonal:
            key_i = lax.broadcasted_iota(jnp.int32, (KV_TILE, 2 * Q_TILE), 0)
            qry_i = lax.broadcasted_iota(jnp.int32, (KV_TILE, 2 * Q_TILE), 1) % Q_TILE
            visible = key_i <= qry_i

        def scores(head):
            k_blk = k_ref[k_rows, head * HEAD_WIDTH:(head + 1) * HEAD_WIDTH]
            return _dot(k_blk, qb_ref[head])

        def fold(head, e, alpha):
            pv = _dot(vt_ref[j, head], e)
            acc_ref[head] = pv if diagonal else alpha * acc_ref[head] + pv

        ahead = [scores(head) for head in range(QK_LOOKAHEAD)]
        pending = None
        for head in range(N_HEADS):
            s = ahead.pop(0)
            if head + QK_LOOKAHEAD < N_HEADS:
                ahead.append(scores(head + QK_LOOKAHEAD))
            alpha = None
            if diagonal:
                s = jnp.where(visible, s, -jnp.inf)
                m_new = jnp.max(s, axis=0, keepdims=True)
            else:
                m_old = m_ref[head]
                m_new = jnp.maximum(m_old, jnp.max(s, axis=0, keepdims=True))
                alpha = jnp.exp2(m_old - m_new)
            m_ref[head] = m_new
            e = jnp.exp2(s - m_new).astype(_BF16)
            if pending is not None:
                fold(*pending)
            pending = (head, e, alpha)
        fold(*pending)

    kv_block(qi, diagonal=True)

    def off_diagonal(j, carry):
        kv_block(j, diagonal=False)
        return carry

    lax.fori_loop(0, qi, off_diagonal, 0)

    lam_p = lam_ref[...]
    lam = (jnp.exp(jnp.sum(lam_p[0:1] * lam_p[1:2], axis=-1, keepdims=True))
           - jnp.exp(jnp.sum(lam_p[2:3] * lam_p[3:4], axis=-1, keepdims=True))
           + lambda_init)
    for head in range(N_HEADS):
        acc = acc_ref[head]
        p = acc[0:HEAD_WIDTH, :] * (1.0 / acc[HEAD_WIDTH:HEAD_WIDTH + 1, :])
        o_h = p[:, :Q_TILE] - lam * p[:, Q_TILE:]
        o_h = (o_h * lax.rsqrt(jnp.mean(o_h * o_h, axis=0, keepdims=True) + EPS)
               * gs_ref[...] * (1.0 - lambda_init))
        heads_ref[:, head * HEAD_WIDTH:(head + 1) * HEAD_WIDTH] = o_h.T.astype(_BF16)

    mixed = _dot(heads_ref[...], wo_ref[...])
    o_ref[...] = x + _rms(mixed, g_ref[1:2, :])


def _diff_attention(x, g2, w_q, lam_params, g_subln_col, w_o, k, vt, layer, j, batch, seq):
    lambda_init = 0.8 - 0.6 * math.exp(-0.3 * layer)
    q_tiles = seq // Q_TILE
    kv_tiles = seq // KV_TILE
    row_spec = pl.BlockSpec((Q_TILE, D_MODEL), lambda b, i: (b * q_tiles + i, 0))
    return pl.pallas_call(
        functools.partial(_attn_kernel, lambda_init),
        out_shape=jax.ShapeDtypeStruct(x.shape, _F32),
        grid=(batch, q_tiles),
        in_specs=[row_spec,
                  _resident((2, D_MODEL), lambda b, i: (0, 0)),
                  _resident((D_MODEL, D_MODEL), lambda b, i: (0, 0)),
                  _resident((None, 4, HEAD_DIM), lambda b, i: (j, 0, 0)),
                  _resident((None, HEAD_WIDTH, 1), lambda b, i: (j, 0, 0)),
                  _resident((D_MODEL, D_MODEL), lambda b, i: (0, 0)),
                  pl.BlockSpec((seq, D_MODEL), lambda b, i: (b, 0)),
                  pl.BlockSpec((kv_tiles, N_HEADS, V_ROWS, KV_TILE), lambda b, i: (b, 0, 0, 0))],
        out_specs=row_spec,
        scratch_shapes=[pltpu.VMEM((N_HEADS, HEAD_WIDTH, 2 * Q_TILE), _BF16),
                        pltpu.VMEM((N_HEADS, 1, 2 * Q_TILE), _F32),
                        pltpu.VMEM((N_HEADS, V_ROWS, 2 * Q_TILE), _F32),
                        pltpu.VMEM((Q_TILE, D_MODEL), _BF16)],
        compiler_params=_params(("parallel", "parallel")),
        name="diff_attention",
    )(x, g2, w_q, lam_params, g_subln_col, w_o, k, vt)


def kernel(x, g_norm, w_ffn_gate, w_ffn_up, w_ffn_down, w_conv_in, w_conv, w_conv_out, g_kv, w_kv,
           w_q, lambda_q1, lambda_k1, lambda_q2, lambda_k2, g_subln, w_o):
    batch, seq, d = x.shape
    assert d == D_MODEL and seq % Q_TILE == 0 and Q_TILE == KV_TILE
    assert seq % ROW_TILE == 0 and seq % CONV_ROW_TILE == 0 and (batch * seq) % FFN_ROW_TILE == 0
    assert sum(w for _, w in FF_CHUNKS) == D_FF

    lam_params = jnp.stack([lambda_q1, lambda_k1, lambda_q2, lambda_k2], axis=1)
    g_subln_col = g_subln[:, :, None]

    def ffn_casts(layer, j):
        return [(w, (layer, j)) for w in (w_ffn_gate, w_ffn_up, w_ffn_down)]

    def mixer_casts(layer):
        if layer < N_A_LAYERS:
            return [(w_conv_in, (layer,)), (w_conv_out, (layer,))]
        return [(w_q, (layer - N_A_LAYERS,)), (w_o, (layer - N_A_LAYERS,))]

    ffn_w = [w[0, 0].astype(_BF16) for w in (w_ffn_gate, w_ffn_up, w_ffn_down)]
    x = x.reshape(batch * seq, d)
    k = vt = None
    for layer in range(DEPTH):
        g = g_norm[layer]
        x, converted = _ffn(x, g[0:2], *ffn_w, casts=ffn_casts(layer, 1) + mixer_casts(layer))
        ffn_w, (w_a, w_b) = converted[:3], converted[3:]
        if layer < N_A_LAYERS:
            x = _conv_mixer(x, g[2:4], w_a, w_conv, w_b, layer, seq)
        else:
            x = _diff_attention(x, g[2:4], w_a, lam_params, g_subln_col, w_b, k, vt,
                                layer, layer - N_A_LAYERS, batch, seq)
        casts = ffn_casts(layer + 1, 0) if layer + 1 < DEPTH else []
        if layer == N_A_LAYERS - 1:
            casts = casts + [(w_kv, ())]
        x, converted = _ffn(x, g[4:6], *ffn_w, casts=casts)
        ffn_w = converted[:3]
        if layer == N_A_LAYERS - 1:
            k, vt = _shared_kv(x, g_kv.reshape(1, d), converted[3])
    return x.reshape(batch, seq, d)
```

```python
import functools
import math

import jax
import jax.numpy as jnp
from jax import lax
from jax.experimental import pallas as pl
from jax.experimental.pallas import tpu as pltpu

D_MODEL = 1024
DEPTH = 4
N_A_LAYERS = DEPTH // 2
D_FF = 2816
CONV_WIDTH = 3
HEAD_DIM = 64
HEAD_WIDTH = 2 * HEAD_DIM
N_HEADS = D_MODEL // HEAD_WIDTH
EPS = 1e-6

ROW_TILE = 512
FFN_ROW_TILE = 1024
FFN_SUB_TILE = 512
CONV_ROW_TILE = 1024
CONV_SUB_TILE = 512
FF_CHUNKS = ((0, 1536), (1536, 1280))
Q_TILE = 256
KV_TILE = 256
QK_LOOKAHEAD = 1
SUBLANES = 8
BF16_SUBLANES = 16
V_ROWS = HEAD_WIDTH + BF16_SUBLANES
VMEM_LIMIT_BYTES = 56 * 1024 * 1024

_BF16 = jnp.bfloat16
_F32 = jnp.float32


def _rms(x, g):
    return x * lax.rsqrt(jnp.mean(x * x, axis=-1, keepdims=True) + EPS) * g


def _dot(a, b):
    return jnp.dot(a, b, preferred_element_type=_F32)


def _resident(shape, index_map):
    return pl.BlockSpec(shape, index_map, pipeline_mode=pl.Buffered(1))


def _params(semantics):
    return pltpu.CompilerParams(dimension_semantics=semantics,
                                vmem_limit_bytes=VMEM_LIMIT_BYTES)


def _ffn_kernel(n_x, n_casts, *refs):
    x_refs, (g_ref, wg_ref, wu_ref, wd_ref) = refs[:n_x], refs[n_x:n_x + 4]
    refs = refs[n_x + 4:]
    src_refs, o_ref, dst_refs = refs[:n_casts], refs[n_casts], refs[n_casts + 1:]
    if n_x == 1:
        subs = [(x_refs[0].at[pl.ds(r * FFN_SUB_TILE, FFN_SUB_TILE)],
                 pl.ds(r * FFN_SUB_TILE, FFN_SUB_TILE)) for r in range(FFN_ROW_TILE // FFN_SUB_TILE)]
    else:
        subs = [(x_ref, pl.ds(r * Q_TILE, Q_TILE)) for r, x_ref in enumerate(x_refs)]
    hs = [_rms(x_sub[...], g_ref[0:1, :]).astype(_BF16) for x_sub, _ in subs]
    for (x_sub, out_rows), h in zip(subs, hs):
        f = None
        for start, width in FF_CHUNKS:
            gate = _dot(h, wg_ref[:, start:start + width])
            up = _dot(h, wu_ref[:, start:start + width])
            act = (gate * jax.nn.sigmoid(gate) * up).astype(_BF16)
            part = _dot(act, wd_ref[start:start + width, :])
            f = part if f is None else f + part
        o_ref[out_rows, :] = x_sub[...] + 0.5 * _rms(f, g_ref[1:2, :])
    for src_ref, dst_ref in zip(src_refs, dst_refs):
        dst_ref[...] = src_ref[...].astype(_BF16)


def _paired_tile(k, n):
    return jnp.where(k % 2 == 0, k // 2, n - 1 - k // 2)


def _paired_position(tile, n):
    return jnp.where(tile < n // 2, 2 * tile, 2 * (n - 1 - tile) + 1)


def _ffn(x, g2, wg, wu, wd, casts=(), gather=None, seq=None):
    t = x.shape[0]
    steps = t // FFN_ROW_TILE
    row_spec = pl.BlockSpec((FFN_ROW_TILE, D_MODEL), lambda i: (i, 0))
    if gather is None:
        x_specs = [row_spec]
    else:
        n = seq // Q_TILE
        per_block = FFN_ROW_TILE // Q_TILE
        blocks_per_seq = n // per_block
        locate = _paired_tile if gather == "to_pairs" else _paired_position

        def tile_index(i, r):
            first = (i // blocks_per_seq) * n
            return first + locate((i % blocks_per_seq) * per_block + r, n), 0

        x_specs = [pl.BlockSpec((Q_TILE, D_MODEL), functools.partial(tile_index, r=r))
                   for r in range(per_block)]
    cast_in_specs, cast_out_specs, cast_shapes = [], [], []
    for w, lead in casts:
        rows, cols = w.shape[len(lead):]
        assert rows % (steps * BF16_SUBLANES) == 0
        cast_in_specs.append(pl.BlockSpec((None,) * len(lead) + (rows // steps, cols),
                                          lambda i, lead=lead: lead + (i, 0)))
        cast_out_specs.append(pl.BlockSpec((rows // steps, cols), lambda i: (i, 0)))
        cast_shapes.append(jax.ShapeDtypeStruct((rows, cols), _BF16))
    out, *converted = pl.pallas_call(
        functools.partial(_ffn_kernel, len(x_specs), len(casts)),
        out_shape=[jax.ShapeDtypeStruct(x.shape, _F32)] + cast_shapes,
        grid=(steps,),
        in_specs=x_specs + [_resident((2, D_MODEL), lambda i: (0, 0)),
                            _resident((D_MODEL, D_FF), lambda i: (0, 0)),
                            _resident((D_MODEL, D_FF), lambda i: (0, 0)),
                            _resident((D_FF, D_MODEL), lambda i: (0, 0))] + cast_in_specs,
        out_specs=[row_spec] + cast_out_specs,
        compiler_params=_params(("parallel",)),
        name="ffn",
    )(*[x] * len(x_specs), g2, wg, wu, wd, *(w for w, _ in casts))
    return out, converted


def _conv_kernel(tiles_per_seq, x_ref, g_ref, w_in_ref, w_conv_ref, w_out_ref, o_ref, tail_ref):
    seq_start = pl.program_id(0) % tiles_per_seq == 0
    tail = jnp.where(seq_start, 0.0, tail_ref[...])
    row = lax.broadcasted_iota(jnp.int32, (CONV_SUB_TILE, D_MODEL), 0)
    for r in range(CONV_ROW_TILE // CONV_SUB_TILE):
        rows = pl.ds(r * CONV_SUB_TILE, CONV_SUB_TILE)
        x = x_ref[rows, :]
        h = _rms(x, g_ref[0:1, :]).astype(_BF16)
        gate_b = _dot(h, w_in_ref[:, 0:D_MODEL])
        gate_c = _dot(h, w_in_ref[:, D_MODEL:2 * D_MODEL])
        u = _dot(h, w_in_ref[:, 2 * D_MODEL:3 * D_MODEL])
        cu = gate_c * u
        prev1 = jnp.where(row == 0, tail[SUBLANES - 1:SUBLANES, :], pltpu.roll(cu, 1, axis=0))
        prev2 = jnp.where(row == 0, tail[SUBLANES - 2:SUBLANES - 1, :],
                          jnp.where(row == 1, tail[SUBLANES - 1:SUBLANES, :],
                                    pltpu.roll(cu, 2, axis=0)))
        tail = cu[CONV_SUB_TILE - SUBLANES:, :]
        z = w_conv_ref[0:1, :] * prev2 + w_conv_ref[1:2, :] * prev1 + w_conv_ref[2:3, :] * cu
        m = _dot((gate_b * z).astype(_BF16), w_out_ref[...])
        o_ref[rows, :] = x + _rms(m, g_ref[1:2, :])
    tail_ref[...] = tail


def _conv_mixer(x, g2, w_in, w_conv, w_out, layer, seq):
    t = x.shape[0]
    row_spec = pl.BlockSpec((CONV_ROW_TILE, D_MODEL), lambda i: (i, 0))
    return pl.pallas_call(
        functools.partial(_conv_kernel, seq // CONV_ROW_TILE),
        out_shape=jax.ShapeDtypeStruct(x.shape, _F32),
        grid=(t // CONV_ROW_TILE,),
        in_specs=[row_spec, _resident((2, D_MODEL), lambda i: (0, 0)),
                  _resident((D_MODEL, 3 * D_MODEL), lambda i: (0, 0)),
                  _resident((None, CONV_WIDTH, D_MODEL), lambda i: (layer, 0, 0)),
                  _resident((D_MODEL, D_MODEL), lambda i: (0, 0))],
        out_specs=row_spec,
        scratch_shapes=[pltpu.VMEM((SUBLANES, D_MODEL), _F32)],
        compiler_params=_params(("arbitrary",)),
        name="conv_mixer",
    )(x, g2, w_in, w_conv, w_out)


def _kv_kernel(x_ref, g_ref, w_ref, k_ref, vt_ref):
    h = _rms(x_ref[...], g_ref[...]).astype(_BF16)
    k_ref[...] = _dot(h, w_ref[:, 0:D_MODEL]).astype(_BF16)
    v = _dot(h, w_ref[:, D_MODEL:2 * D_MODEL])
    ones = jnp.ones((V_ROWS - HEAD_WIDTH, KV_TILE), _BF16)
    for r in range(ROW_TILE // KV_TILE):
        v_t = v[r * KV_TILE:(r + 1) * KV_TILE, :].T.astype(_BF16)
        for head in range(N_HEADS):
            vt_ref[r, head, 0:HEAD_WIDTH, :] = v_t[head * HEAD_WIDTH:(head + 1) * HEAD_WIDTH, :]
            vt_ref[r, head, HEAD_WIDTH:V_ROWS, :] = ones


def _shared_kv(x, g_kv, w_kv):
    t = x.shape[0]
    per_step = ROW_TILE // KV_TILE
    return pl.pallas_call(
        _kv_kernel,
        out_shape=(jax.ShapeDtypeStruct((t, D_MODEL), _BF16),
                   jax.ShapeDtypeStruct((t // KV_TILE, N_HEADS, V_ROWS, KV_TILE), _BF16)),
        grid=(t // ROW_TILE,),
        in_specs=[pl.BlockSpec((ROW_TILE, D_MODEL), lambda i: (i, 0)),
                  _resident((1, D_MODEL), lambda i: (0, 0)),
                  _resident((D_MODEL, 2 * D_MODEL), lambda i: (0, 0))],
        out_specs=(pl.BlockSpec((ROW_TILE, D_MODEL), lambda i: (i, 0)),
                   pl.BlockSpec((per_step, N_HEADS, V_ROWS, KV_TILE), lambda i: (i, 0, 0, 0))),
        compiler_params=_params(("parallel",)),
        name="shared_kv",
    )(x, g_kv, w_kv)


def _attn_kernel(lambda_init, n_tiles, x_ref, g_ref, wq_ref, lam_ref, gs_ref, wo_ref, k_ref, vt_ref,
                 o_ref, qb_ref, m_ref, acc_ref, heads_ref):
    p = pl.program_id(1)
    x = x_ref[...]
    h = _rms(x, g_ref[0:1, :]).astype(_BF16)
    q_scale = HEAD_DIM ** -0.5 * math.log2(math.e)
    q_t = (_dot(h, wq_ref[...]) * q_scale).T.astype(_BF16)

    feat = lax.broadcasted_iota(jnp.int32, (HEAD_WIDTH, Q_TILE), 0)
    zero = jnp.zeros((), _BF16)
    for side in range(2):
        for head in range(N_HEADS):
            q_h = q_t[head * HEAD_WIDTH:(head + 1) * HEAD_WIDTH, side * Q_TILE:(side + 1) * Q_TILE]
            qb_ref[side * N_HEADS + head, :, 0:Q_TILE] = jnp.where(feat < HEAD_DIM, q_h, zero)
            qb_ref[side * N_HEADS + head, :, Q_TILE:2 * Q_TILE] = jnp.where(feat >= HEAD_DIM, q_h, zero)

    key_i = lax.broadcasted_iota(jnp.int32, (KV_TILE, 2 * Q_TILE), 0)
    qry_i = lax.broadcasted_iota(jnp.int32, (KV_TILE, 2 * Q_TILE), 1) % Q_TILE
    visible = key_i <= qry_i

    blocks = [(0, p, True), (1, n_tiles - 1 - p, True)]
    for u in range(n_tiles - 1):
        first_side = u < p
        blocks.append((jnp.where(first_side, 0, 1), jnp.where(first_side, u, u - p), False))
    items = [(side, j, diagonal, head) for side, j, diagonal in blocks for head in range(N_HEADS)]

    def scores(item):
        side, j, _, head = item
        k_rows = pl.ds(pl.multiple_of(j * KV_TILE, KV_TILE), KV_TILE)
        k_blk = k_ref[k_rows, head * HEAD_WIDTH:(head + 1) * HEAD_WIDTH]
        return _dot(k_blk, qb_ref[side * N_HEADS + head])

    def fold(item, e, alpha):
        side, j, diagonal, head = item
        pv = _dot(vt_ref[j, head], e)
        state = side * N_HEADS + head
        acc_ref[state] = pv if diagonal else alpha * acc_ref[state] + pv

    ahead = [scores(item) for item in items[:QK_LOOKAHEAD]]
    pending = None
    for n, item in enumerate(items):
        side, j, diagonal, head = item
        state = side * N_HEADS + head
        s = ahead.pop(0)
        if n + QK_LOOKAHEAD < len(items):
            ahead.append(scores(items[n + QK_LOOKAHEAD]))
        alpha = None
        if diagonal:
            s = jnp.where(visible, s, -jnp.inf)
            m_new = jnp.max(s, axis=0, keepdims=True)
        else:
            m_old = m_ref[state]
            m_new = jnp.maximum(m_old, jnp.max(s, axis=0, keepdims=True))
            alpha = jnp.exp2(m_old - m_new)
        m_ref[state] = m_new
        e = jnp.exp2(s - m_new).astype(_BF16)
        if pending is not None:
            fold(*pending)
        pending = (item, e, alpha)
    fold(*pending)

    lam_p = lam_ref[...]
    lam = (jnp.exp(jnp.sum(lam_p[0:1] * lam_p[1:2], axis=-1, keepdims=True))
           - jnp.exp(jnp.sum(lam_p[2:3] * lam_p[3:4], axis=-1, keepdims=True))
           + lambda_init)
    for side in range(2):
        for head in range(N_HEADS):
            acc = acc_ref[side * N_HEADS + head]
            pr = acc[0:HEAD_WIDTH, :] * (1.0 / acc[HEAD_WIDTH:HEAD_WIDTH + 1, :])
            o_h = pr[:, :Q_TILE] - lam * pr[:, Q_TILE:]
            o_h = (o_h * lax.rsqrt(jnp.mean(o_h * o_h, axis=0, keepdims=True) + EPS)
                   * gs_ref[...] * (1.0 - lambda_init))
            heads_ref[side * Q_TILE:(side + 1) * Q_TILE,
                      head * HEAD_WIDTH:(head + 1) * HEAD_WIDTH] = o_h.T.astype(_BF16)

    mixed = _dot(heads_ref[...], wo_ref[...])
    o_ref[...] = x + _rms(mixed, g_ref[1:2, :])


def _diff_attention(x, g2, w_q, lam_params, g_subln_col, w_o, k, vt, layer, j, batch, seq):
    lambda_init = 0.8 - 0.6 * math.exp(-0.3 * layer)
    n_tiles = seq // Q_TILE
    pairs = n_tiles // 2
    kv_tiles = seq // KV_TILE
    row_spec = pl.BlockSpec((2 * Q_TILE, D_MODEL), lambda b, i: (b * pairs + i, 0))
    return pl.pallas_call(
        functools.partial(_attn_kernel, lambda_init, n_tiles),
        out_shape=jax.ShapeDtypeStruct(x.shape, _F32),
        grid=(batch, pairs),
        in_specs=[row_spec,
                  _resident((2, D_MODEL), lambda b, i: (0, 0)),
                  _resident((D_MODEL, D_MODEL), lambda b, i: (0, 0)),
                  _resident((None, 4, HEAD_DIM), lambda b, i: (j, 0, 0)),
                  _resident((None, HEAD_WIDTH, 1), lambda b, i: (j, 0, 0)),
                  _resident((D_MODEL, D_MODEL), lambda b, i: (0, 0)),
                  pl.BlockSpec((seq, D_MODEL), lambda b, i: (b, 0)),
                  pl.BlockSpec((kv_tiles, N_HEADS, V_ROWS, KV_TILE), lambda b, i: (b, 0, 0, 0))],
        out_specs=row_spec,
        scratch_shapes=[pltpu.VMEM((2 * N_HEADS, HEAD_WIDTH, 2 * Q_TILE), _BF16),
                        pltpu.VMEM((2 * N_HEADS, 1, 2 * Q_TILE), _F32),
                        pltpu.VMEM((2 * N_HEADS, V_ROWS, 2 * Q_TILE), _F32),
                        pltpu.VMEM((2 * Q_TILE, D_MODEL), _BF16)],
        compiler_params=_params(("parallel", "parallel")),
        name="diff_attention",
    )(x, g2, w_q, lam_params, g_subln_col, w_o, k, vt)


def kernel(x, g_norm, w_ffn_gate, w_ffn_up, w_ffn_down, w_conv_in, w_conv, w_conv_out, g_kv, w_kv,
           w_q, lambda_q1, lambda_k1, lambda_q2, lambda_k2, g_subln, w_o):
    batch, seq, d = x.shape
    assert d == D_MODEL and seq % (2 * Q_TILE) == 0 and Q_TILE == KV_TILE
    assert FFN_ROW_TILE % Q_TILE == 0 and (seq // Q_TILE) % (FFN_ROW_TILE // Q_TILE) == 0
    assert seq % ROW_TILE == 0 and seq % CONV_ROW_TILE == 0 and (batch * seq) % FFN_ROW_TILE == 0
    assert sum(w for _, w in FF_CHUNKS) == D_FF

    lam_params = jnp.stack([lambda_q1, lambda_k1, lambda_q2, lambda_k2], axis=1)
    g_subln_col = g_subln[:, :, None]

    def ffn_casts(layer, j):
        return [(w, (layer, j)) for w in (w_ffn_gate, w_ffn_up, w_ffn_down)]

    def mixer_casts(layer):
        if layer < N_A_LAYERS:
            return [(w_conv_in, (layer,)), (w_conv_out, (layer,))]
        return [(w_q, (layer - N_A_LAYERS,)), (w_o, (layer - N_A_LAYERS,))]

    ffn_w = [w[0, 0].astype(_BF16) for w in (w_ffn_gate, w_ffn_up, w_ffn_down)]
    x = x.reshape(batch * seq, d)
    k = vt = None
    for layer in range(DEPTH):
        g = g_norm[layer]
        gather = "to_pairs" if layer == N_A_LAYERS else None
        x, converted = _ffn(x, g[0:2], *ffn_w, casts=ffn_casts(layer, 1) + mixer_casts(layer),
                            gather=gather, seq=seq)
        ffn_w, (w_a, w_b) = converted[:3], converted[3:]
        if layer < N_A_LAYERS:
            x = _conv_mixer(x, g[2:4], w_a, w_conv, w_b, layer, seq)
        else:
            x = _diff_attention(x, g[2:4], w_a, lam_params, g_subln_col, w_b, k, vt,
                                layer, layer - N_A_LAYERS, batch, seq)
        casts = ffn_casts(layer + 1, 0) if layer + 1 < DEPTH else []
        if layer == N_A_LAYERS - 1:
            casts = casts + [(w_kv, ())]
        gather = "from_pairs" if layer == DEPTH - 1 else None
        x, converted = _ffn(x, g[4:6], *ffn_w, casts=casts, gather=gather, seq=seq)
        ffn_w = converted[:3]
        if layer == N_A_LAYERS - 1:
            k, vt = _shared_kv(x, g_kv.reshape(1, d), converted[3])
    return x.reshape(batch, seq, d)
```

```python
import functools
import math

import jax
import jax.numpy as jnp
from jax import lax
from jax.experimental import pallas as pl
from jax.experimental.pallas import tpu as pltpu

D_MODEL = 1024
DEPTH = 4
N_A_LAYERS = DEPTH // 2
D_FF = 2816
CONV_WIDTH = 3
HEAD_DIM = 64
HEAD_WIDTH = 2 * HEAD_DIM
N_HEADS = D_MODEL // HEAD_WIDTH
EPS = 1e-6

ROW_TILE = 512
FFN_ROW_TILE = 1024
FFN_SUB_TILE = 256
CONV_ROW_TILE = 1024
CONV_SUB_TILE = 512
FF_CHUNKS = ((0, 1536), (1536, 1280))
Q_TILE = 256
KV_TILE = 256
QK_LOOKAHEAD = 1
PV_DELAY = 1
SUBLANES = 8
BF16_SUBLANES = 16
V_ROWS = HEAD_WIDTH + BF16_SUBLANES
VMEM_LIMIT_BYTES = 56 * 1024 * 1024

_BF16 = jnp.bfloat16
_F32 = jnp.float32


def _rms(x, g, scale=None):
    r = lax.rsqrt(jnp.mean(x * x, axis=-1, keepdims=True) + EPS)
    return x * (r if scale is None else r * scale) * g


def _dot(a, b):
    return jnp.dot(a, b, preferred_element_type=_F32)


def _resident(shape, index_map):
    return pl.BlockSpec(shape, index_map, pipeline_mode=pl.Buffered(1))


def _params(semantics):
    return pltpu.CompilerParams(dimension_semantics=semantics,
                                vmem_limit_bytes=VMEM_LIMIT_BYTES)


def _ffn_kernel(n_x, n_casts, *refs):
    x_refs, (g_ref, wg_ref, wu_ref, wd_ref) = refs[:n_x], refs[n_x:n_x + 4]
    refs = refs[n_x + 4:]
    src_refs, o_ref, dst_refs = refs[:n_casts], refs[n_casts], refs[n_casts + 1:]
    if n_x == 1:
        subs = [(x_refs[0].at[pl.ds(r * FFN_SUB_TILE, FFN_SUB_TILE)],
                 pl.ds(r * FFN_SUB_TILE, FFN_SUB_TILE)) for r in range(FFN_ROW_TILE // FFN_SUB_TILE)]
    else:
        subs = [(x_ref, pl.ds(r * Q_TILE, Q_TILE)) for r, x_ref in enumerate(x_refs)]
    hs = [_rms(x_sub[...], g_ref[0:1, :]).astype(_BF16) for x_sub, _ in subs]
    for (x_sub, out_rows), h in zip(subs, hs):
        f = None
        for start, width in FF_CHUNKS:
            gate = _dot(h, wg_ref[:, start:start + width])
            up = _dot(h, wu_ref[:, start:start + width])
            act = (gate * jax.nn.sigmoid(gate) * up).astype(_BF16)
            part = _dot(act, wd_ref[start:start + width, :])
            f = part if f is None else f + part
        o_ref[out_rows, :] = x_sub[...] + _rms(f, g_ref[1:2, :], scale=0.5)
    for src_ref, dst_ref in zip(src_refs, dst_refs):
        dst_ref[...] = src_ref[...].astype(_BF16)


def _paired_tile(k, n):
    return jnp.where(k % 2 == 0, k // 2, n - 1 - k // 2)


def _paired_position(tile, n):
    return jnp.where(tile < n // 2, 2 * tile, 2 * (n - 1 - tile) + 1)


def _ffn(x, g2, wg, wu, wd, casts=(), gather=None, seq=None):
    t = x.shape[0]
    steps = t // FFN_ROW_TILE
    row_spec = pl.BlockSpec((FFN_ROW_TILE, D_MODEL), lambda i: (i, 0))
    if gather is None:
        x_specs = [row_spec]
    else:
        n = seq // Q_TILE
        per_block = FFN_ROW_TILE // Q_TILE
        blocks_per_seq = n // per_block
        locate = _paired_tile if gather == "to_pairs" else _paired_position

        def tile_index(i, r):
            first = (i // blocks_per_seq) * n
            return first + locate((i % blocks_per_seq) * per_block + r, n), 0

        x_specs = [pl.BlockSpec((Q_TILE, D_MODEL), functools.partial(tile_index, r=r))
                   for r in range(per_block)]
    cast_in_specs, cast_out_specs, cast_shapes = [], [], []
    for w, lead in casts:
        rows, cols = w.shape[len(lead):]
        assert rows % (steps * BF16_SUBLANES) == 0
        cast_in_specs.append(pl.BlockSpec((None,) * len(lead) + (rows // steps, cols),
                                          lambda i, lead=lead: lead + (i, 0)))
        cast_out_specs.append(pl.BlockSpec((rows // steps, cols), lambda i: (i, 0)))
        cast_shapes.append(jax.ShapeDtypeStruct((rows, cols), _BF16))
    out, *converted = pl.pallas_call(
        functools.partial(_ffn_kernel, len(x_specs), len(casts)),
        out_shape=[jax.ShapeDtypeStruct(x.shape, _F32)] + cast_shapes,
        grid=(steps,),
        in_specs=x_specs + [_resident((2, D_MODEL), lambda i: (0, 0)),
                            _resident((D_MODEL, D_FF), lambda i: (0, 0)),
                            _resident((D_MODEL, D_FF), lambda i: (0, 0)),
                            _resident((D_FF, D_MODEL), lambda i: (0, 0))] + cast_in_specs,
        out_specs=[row_spec] + cast_out_specs,
        compiler_params=_params(("parallel",)),
        name="ffn",
    )(*[x] * len(x_specs), g2, wg, wu, wd, *(w for w, _ in casts))
    return out, converted


def _conv_kernel(tiles_per_seq, x_ref, g_ref, w_in_ref, w_conv_ref, w_out_ref, o_ref, tail_ref):
    seq_start = pl.program_id(0) % tiles_per_seq == 0
    tail = jnp.where(seq_start, 0.0, tail_ref[...])
    row = lax.broadcasted_iota(jnp.int32, (SUBLANES, D_MODEL), 0)
    for r in range(CONV_ROW_TILE // CONV_SUB_TILE):
        rows = pl.ds(r * CONV_SUB_TILE, CONV_SUB_TILE)
        x = x_ref[rows, :]
        h = _rms(x, g_ref[0:1, :]).astype(_BF16)
        gate_b = _dot(h, w_in_ref[:, 0:D_MODEL])
        gate_c = _dot(h, w_in_ref[:, D_MODEL:2 * D_MODEL])
        u = _dot(h, w_in_ref[:, 2 * D_MODEL:3 * D_MODEL])
        cu = gate_c * u

        def taps(cur, prev1, prev2):
            return (w_conv_ref[0:1, :] * prev2 + w_conv_ref[1:2, :] * prev1
                    + w_conv_ref[2:3, :] * cur)

        z = taps(cu, pltpu.roll(cu, 1, axis=0), pltpu.roll(cu, 2, axis=0))
        top = cu[0:SUBLANES, :]
        top1 = jnp.where(row == 0, tail[SUBLANES - 1:SUBLANES, :], pltpu.roll(top, 1, axis=0))
        top2 = jnp.where(row == 0, tail[SUBLANES - 2:SUBLANES - 1, :],
                         jnp.where(row == 1, tail[SUBLANES - 1:SUBLANES, :],
                                   pltpu.roll(top, 2, axis=0)))
        z = jnp.concatenate([taps(top, top1, top2), z[SUBLANES:, :]], axis=0)
        tail = cu[CONV_SUB_TILE - SUBLANES:, :]
        m = _dot((gate_b * z).astype(_BF16), w_out_ref[...])
        o_ref[rows, :] = x + _rms(m, g_ref[1:2, :])
    tail_ref[...] = tail


def _conv_mixer(x, g2, w_in, w_conv, w_out, layer, seq):
    t = x.shape[0]
    row_spec = pl.BlockSpec((CONV_ROW_TILE, D_MODEL), lambda i: (i, 0))
    return pl.pallas_call(
        functools.partial(_conv_kernel, seq // CONV_ROW_TILE),
        out_shape=jax.ShapeDtypeStruct(x.shape, _F32),
        grid=(t // CONV_ROW_TILE,),
        in_specs=[row_spec, _resident((2, D_MODEL), lambda i: (0, 0)),
                  _resident((D_MODEL, 3 * D_MODEL), lambda i: (0, 0)),
                  _resident((None, CONV_WIDTH, D_MODEL), lambda i: (layer, 0, 0)),
                  _resident((D_MODEL, D_MODEL), lambda i: (0, 0))],
        out_specs=row_spec,
        scratch_shapes=[pltpu.VMEM((SUBLANES, D_MODEL), _F32)],
        compiler_params=_params(("arbitrary",)),
        name="conv_mixer",
    )(x, g2, w_in, w_conv, w_out)


def _kv_kernel(x_ref, g_ref, w_ref, k_ref, vt_ref):
    h = _rms(x_ref[...], g_ref[...]).astype(_BF16)
    k_ref[...] = _dot(h, w_ref[:, 0:D_MODEL]).astype(_BF16)
    v = _dot(h, w_ref[:, D_MODEL:2 * D_MODEL])
    ones = jnp.ones((V_ROWS - HEAD_WIDTH, KV_TILE), _BF16)
    for r in range(ROW_TILE // KV_TILE):
        v_t = v[r * KV_TILE:(r + 1) * KV_TILE, :].T.astype(_BF16)
        for head in range(N_HEADS):
            vt_ref[r, head, 0:HEAD_WIDTH, :] = v_t[head * HEAD_WIDTH:(head + 1) * HEAD_WIDTH, :]
            vt_ref[r, head, HEAD_WIDTH:V_ROWS, :] = ones


def _shared_kv(x, g_kv, w_kv):
    t = x.shape[0]
    per_step = ROW_TILE // KV_TILE
    return pl.pallas_call(
        _kv_kernel,
        out_shape=(jax.ShapeDtypeStruct((t, D_MODEL), _BF16),
                   jax.ShapeDtypeStruct((t // KV_TILE, N_HEADS, V_ROWS, KV_TILE), _BF16)),
        grid=(t // ROW_TILE,),
        in_specs=[pl.BlockSpec((ROW_TILE, D_MODEL), lambda i: (i, 0)),
                  _resident((1, D_MODEL), lambda i: (0, 0)),
                  _resident((D_MODEL, 2 * D_MODEL), lambda i: (0, 0))],
        out_specs=(pl.BlockSpec((ROW_TILE, D_MODEL), lambda i: (i, 0)),
                   pl.BlockSpec((per_step, N_HEADS, V_ROWS, KV_TILE), lambda i: (i, 0, 0, 0))),
        compiler_params=_params(("parallel",)),
        name="shared_kv",
    )(x, g_kv, w_kv)


def _attn_kernel(lambda_init, n_tiles, x_ref, g_ref, wq_ref, lam_ref, gs_ref, wo_ref, k_ref, vt_ref,
                 o_ref, qb_ref, m_ref, acc_ref, heads_ref):
    p = pl.program_id(1)
    x = x_ref[...]
    h = _rms(x, g_ref[0:1, :]).astype(_BF16)
    q_scale = HEAD_DIM ** -0.5 * math.log2(math.e)
    q_t = (_dot(h, wq_ref[...]) * q_scale).T.astype(_BF16)

    feat = lax.broadcasted_iota(jnp.int32, (HEAD_WIDTH, Q_TILE), 0)
    zero = jnp.zeros((), _BF16)
    for side in range(2):
        for head in range(N_HEADS):
            q_h = q_t[head * HEAD_WIDTH:(head + 1) * HEAD_WIDTH, side * Q_TILE:(side + 1) * Q_TILE]
            qb_ref[side * N_HEADS + head, :, 0:Q_TILE] = jnp.where(feat < HEAD_DIM, q_h, zero)
            qb_ref[side * N_HEADS + head, :, Q_TILE:2 * Q_TILE] = jnp.where(feat >= HEAD_DIM, q_h, zero)

    key_i = lax.broadcasted_iota(jnp.int32, (KV_TILE, 2 * Q_TILE), 0)
    qry_i = lax.broadcasted_iota(jnp.int32, (KV_TILE, 2 * Q_TILE), 1) % Q_TILE
    visible = key_i <= qry_i

    blocks = [(0, p, True), (1, n_tiles - 1 - p, True)]
    for u in range(n_tiles - 1):
        first_side = u < p
        blocks.append((jnp.where(first_side, 0, 1), jnp.where(first_side, u, u - p), False))
    items = [(side, j, diagonal, head) for side, j, diagonal in blocks for head in range(N_HEADS)]

    def scores(item):
        side, j, _, head = item
        k_rows = pl.ds(pl.multiple_of(j * KV_TILE, KV_TILE), KV_TILE)
        k_blk = k_ref[k_rows, head * HEAD_WIDTH:(head + 1) * HEAD_WIDTH]
        return _dot(k_blk, qb_ref[side * N_HEADS + head])

    def fold(item, e, alpha):
        side, j, diagonal, head = item
        pv = _dot(vt_ref[j, head], e)
        state = side * N_HEADS + head
        acc_ref[state] = pv if diagonal else alpha * acc_ref[state] + pv

    ahead = [scores(item) for item in items[:QK_LOOKAHEAD]]
    pending = []
    for n, item in enumerate(items):
        side, j, diagonal, head = item
        state = side * N_HEADS + head
        s = ahead.pop(0)
        if n + QK_LOOKAHEAD < len(items):
            ahead.append(scores(items[n + QK_LOOKAHEAD]))
        alpha = None
        if diagonal:
            s = jnp.where(visible, s, -jnp.inf)
            m_new = jnp.max(s, axis=0, keepdims=True)
        else:
            m_old = m_ref[state]
            m_new = jnp.maximum(m_old, jnp.max(s, axis=0, keepdims=True))
            alpha = jnp.exp2(m_old - m_new)
        m_ref[state] = m_new
        e = jnp.exp2(s - m_new).astype(_BF16)
        pending.append((item, e, alpha))
        if len(pending) > PV_DELAY:
            fold(*pending.pop(0))
    for work in pending:
        fold(*work)

    lam_p = lam_ref[...]
    lam = (jnp.exp(jnp.sum(lam_p[0:1] * lam_p[1:2], axis=-1, keepdims=True))
           - jnp.exp(jnp.sum(lam_p[2:3] * lam_p[3:4], axis=-1, keepdims=True))
           + lambda_init)
    gain = gs_ref[...] * (1.0 - lambda_init)
    for side in range(2):
        for head in range(N_HEADS):
            acc = acc_ref[side * N_HEADS + head]
            inv_l = 1.0 / acc[HEAD_WIDTH:HEAD_WIDTH + 1, :]
            o_h = (acc[0:HEAD_WIDTH, :Q_TILE] * inv_l[:, :Q_TILE]
                   - acc[0:HEAD_WIDTH, Q_TILE:] * (lam * inv_l[:, Q_TILE:]))
            o_h = o_h * lax.rsqrt(jnp.mean(o_h * o_h, axis=0, keepdims=True) + EPS) * gain
            heads_ref[side * Q_TILE:(side + 1) * Q_TILE,
                      head * HEAD_WIDTH:(head + 1) * HEAD_WIDTH] = o_h.T.astype(_BF16)

    mixed = _dot(heads_ref[...], wo_ref[...])
    o_ref[...] = x + _rms(mixed, g_ref[1:2, :])


def _diff_attention(x, g2, w_q, lam_params, g_subln_col, w_o, k, vt, layer, j, batch, seq):
    lambda_init = 0.8 - 0.6 * math.exp(-0.3 * layer)
    n_tiles = seq // Q_TILE
    pairs = n_tiles // 2
    kv_tiles = seq // KV_TILE
    row_spec = pl.BlockSpec((2 * Q_TILE, D_MODEL), lambda b, i: (b * pairs + i, 0))
    return pl.pallas_call(
        functools.partial(_attn_kernel, lambda_init, n_tiles),
        out_shape=jax.ShapeDtypeStruct(x.shape, _F32),
        grid=(batch, pairs),
        in_specs=[row_spec,
                  _resident((2, D_MODEL), lambda b, i: (0, 0)),
                  _resident((D_MODEL, D_MODEL), lambda b, i: (0, 0)),
                  _resident((None, 4, HEAD_DIM), lambda b, i: (j, 0, 0)),
                  _resident((None, HEAD_WIDTH, 1), lambda b, i: (j, 0, 0)),
                  _resident((D_MODEL, D_MODEL), lambda b, i: (0, 0)),
                  pl.BlockSpec((seq, D_MODEL), lambda b, i: (b, 0)),
                  pl.BlockSpec((kv_tiles, N_HEADS, V_ROWS, KV_TILE), lambda b, i: (b, 0, 0, 0))],
        out_specs=row_spec,
        scratch_shapes=[pltpu.VMEM((2 * N_HEADS, HEAD_WIDTH, 2 * Q_TILE), _BF16),
                        pltpu.VMEM((2 * N_HEADS, 1, 2 * Q_TILE), _F32),
                        pltpu.VMEM((2 * N_HEADS, V_ROWS, 2 * Q_TILE), _F32),
                        pltpu.VMEM((2 * Q_TILE, D_MODEL), _BF16)],
        compiler_params=_params(("parallel", "parallel")),
        name="diff_attention",
    )(x, g2, w_q, lam_params, g_subln_col, w_o, k, vt)


def kernel(x, g_norm, w_ffn_gate, w_ffn_up, w_ffn_down, w_conv_in, w_conv, w_conv_out, g_kv, w_kv,
           w_q, lambda_q1, lambda_k1, lambda_q2, lambda_k2, g_subln, w_o):
    batch, seq, d = x.shape
    assert d == D_MODEL and seq % (2 * Q_TILE) == 0 and Q_TILE == KV_TILE
    assert FFN_ROW_TILE % Q_TILE == 0 and (seq // Q_TILE) % (FFN_ROW_TILE // Q_TILE) == 0
    assert seq % ROW_TILE == 0 and seq % CONV_ROW_TILE == 0 and (batch * seq) % FFN_ROW_TILE == 0
    assert sum(w for _, w in FF_CHUNKS) == D_FF

    lam_params = jnp.stack([lambda_q1, lambda_k1, lambda_q2, lambda_k2], axis=1)
    g_subln_col = g_subln[:, :, None]

    def ffn_casts(layer, j):
        return [(w, (layer, j)) for w in (w_ffn_gate, w_ffn_up, w_ffn_down)]

    def mixer_casts(layer):
        if layer < N_A_LAYERS:
            return [(w_conv_in, (layer,)), (w_conv_out, (layer,))]
        return [(w_q, (layer - N_A_LAYERS,)), (w_o, (layer - N_A_LAYERS,))]

    ffn_w = [w[0, 0].astype(_BF16) for w in (w_ffn_gate, w_ffn_up, w_ffn_down)]
    x = x.reshape(batch * seq, d)
    k = vt = None
    for layer in range(DEPTH):
        g = g_norm[layer]
        gather = "to_pairs" if layer == N_A_LAYERS else None
        x, converted = _ffn(x, g[0:2], *ffn_w, casts=ffn_casts(layer, 1) + mixer_casts(layer),
                            gather=gather, seq=seq)
        ffn_w, (w_a, w_b) = converted[:3], converted[3:]
        if layer < N_A_LAYERS:
            x = _conv_mixer(x, g[2:4], w_a, w_conv, w_b, layer, seq)
        else:
            x = _diff_attention(x, g[2:4], w_a, lam_params, g_subln_col, w_b, k, vt,
                                layer, layer - N_A_LAYERS, batch, seq)
        casts = ffn_casts(layer + 1, 0) if layer + 1 < DEPTH else []
        if layer == N_A_LAYERS - 1:
            casts = casts + [(w_kv, ())]
        gather = "from_pairs" if layer == DEPTH - 1 else None
        x, converted = _ffn(x, g[4:6], *ffn_w, casts=casts, gather=gather, seq=seq)
        ffn_w = converted[:3]
        if layer == N_A_LAYERS - 1:
            k, vt = _shared_kv(x, g_kv.reshape(1, d), converted[3])
    return x.reshape(batch, seq, d)
```

```python
import functools
import math

import jax
import jax.numpy as jnp
from jax import lax
from jax.experimental import pallas as pl
from jax.experimental.pallas import tpu as pltpu

D_MODEL = 1024
DEPTH = 4
N_A_LAYERS = DEPTH // 2
D_FF = 2816
CONV_WIDTH = 3
HEAD_DIM = 64
HEAD_WIDTH = 2 * HEAD_DIM
N_HEADS = D_MODEL // HEAD_WIDTH
EPS = 1e-6

ROW_TILE = 1024
FFN_ROW_TILE = 1024
FFN_SUB_TILE = 256
CONV_ROW_TILE = 1024
CONV_SUB_TILE = 256
FF_CHUNKS = ((0, 1536), (1536, 1280))
Q_TILE = 256
KV_TILE = 256
QK_LOOKAHEAD = 1
PV_DELAY = 1
SUBLANES = 8
BF16_SUBLANES = 16
V_ROWS = HEAD_WIDTH + BF16_SUBLANES
VMEM_LIMIT_BYTES = 56 * 1024 * 1024

_BF16 = jnp.bfloat16
_F32 = jnp.float32


def _rms(x, g, scale=None):
    r = lax.rsqrt(jnp.mean(x * x, axis=-1, keepdims=True) + EPS)
    return x * (r if scale is None else r * scale) * g


def _dot(a, b):
    return jnp.dot(a, b, preferred_element_type=_F32)


def _resident(shape, index_map):
    return pl.BlockSpec(shape, index_map, pipeline_mode=pl.Buffered(1))


def _params(semantics):
    return pltpu.CompilerParams(dimension_semantics=semantics,
                                vmem_limit_bytes=VMEM_LIMIT_BYTES)


def _ffn_kernel(n_x, n_casts, *refs):
    x_refs, (g_ref, wg_ref, wu_ref, wd_ref) = refs[:n_x], refs[n_x:n_x + 4]
    refs = refs[n_x + 4:]
    src_refs, o_ref, dst_refs = refs[:n_casts], refs[n_casts], refs[n_casts + 1:]
    if n_x == 1:
        subs = [(x_refs[0].at[pl.ds(r * FFN_SUB_TILE, FFN_SUB_TILE)],
                 pl.ds(r * FFN_SUB_TILE, FFN_SUB_TILE)) for r in range(FFN_ROW_TILE // FFN_SUB_TILE)]
    else:
        subs = [(x_ref, pl.ds(r * Q_TILE, Q_TILE)) for r, x_ref in enumerate(x_refs)]
    hs = [_rms(x_sub[...], g_ref[0:1, :]).astype(_BF16) for x_sub, _ in subs]
    for (x_sub, out_rows), h in zip(subs, hs):
        f = None
        for start, width in FF_CHUNKS:
            gate = _dot(h, wg_ref[:, start:start + width])
            up = _dot(h, wu_ref[:, start:start + width])
            act = (gate * jax.nn.sigmoid(gate) * up).astype(_BF16)
            part = _dot(act, wd_ref[start:start + width, :])
            f = part if f is None else f + part
        o_ref[out_rows, :] = x_sub[...] + _rms(f, g_ref[1:2, :], scale=0.5)
    for src_ref, dst_ref in zip(src_refs, dst_refs):
        dst_ref[...] = src_ref[...].astype(_BF16)


def _paired_tile(k, n):
    return jnp.where(k % 2 == 0, k // 2, n - 1 - k // 2)


def _paired_position(tile, n):
    return jnp.where(tile < n // 2, 2 * tile, 2 * (n - 1 - tile) + 1)


def _ffn(x, g2, wg, wu, wd, casts=(), gather=None, seq=None):
    t = x.shape[0]
    steps = t // FFN_ROW_TILE
    row_spec = pl.BlockSpec((FFN_ROW_TILE, D_MODEL), lambda i: (i, 0))
    if gather is None:
        x_specs = [row_spec]
    else:
        n = seq // Q_TILE
        per_block = FFN_ROW_TILE // Q_TILE
        blocks_per_seq = n // per_block
        locate = _paired_tile if gather == "to_pairs" else _paired_position

        def tile_index(i, r):
            first = (i // blocks_per_seq) * n
            return first + locate((i % blocks_per_seq) * per_block + r, n), 0

        x_specs = [pl.BlockSpec((Q_TILE, D_MODEL), functools.partial(tile_index, r=r))
                   for r in range(per_block)]
    cast_in_specs, cast_out_specs, cast_shapes = [], [], []
    for w, lead in casts:
        rows, cols = w.shape[len(lead):]
        assert rows % (steps * BF16_SUBLANES) == 0
        cast_in_specs.append(pl.BlockSpec((None,) * len(lead) + (rows // steps, cols),
                                          lambda i, lead=lead: lead + (i, 0)))
        cast_out_specs.append(pl.BlockSpec((rows // steps, cols), lambda i: (i, 0)))
        cast_shapes.append(jax.ShapeDtypeStruct((rows, cols), _BF16))
    out, *converted = pl.pallas_call(
        functools.partial(_ffn_kernel, len(x_specs), len(casts)),
        out_shape=[jax.ShapeDtypeStruct(x.shape, _F32)] + cast_shapes,
        grid=(steps,),
        in_specs=x_specs + [_resident((2, D_MODEL), lambda i: (0, 0)),
                            _resident((D_MODEL, D_FF), lambda i: (0, 0)),
                            _resident((D_MODEL, D_FF), lambda i: (0, 0)),
                            _resident((D_FF, D_MODEL), lambda i: (0, 0))] + cast_in_specs,
        out_specs=[row_spec] + cast_out_specs,
        compiler_params=_params(("parallel",)),
        name="ffn",
    )(*[x] * len(x_specs), g2, wg, wu, wd, *(w for w, _ in casts))
    return out, converted


def _conv_kernel(tiles_per_seq, x_ref, g_ref, w_in_ref, w_conv_ref, w_out_ref, o_ref, tail_ref):
    seq_start = pl.program_id(0) % tiles_per_seq == 0
    tail = jnp.where(seq_start, 0.0, tail_ref[...])
    row = lax.broadcasted_iota(jnp.int32, (SUBLANES, D_MODEL), 0)
    for r in range(CONV_ROW_TILE // CONV_SUB_TILE):
        rows = pl.ds(r * CONV_SUB_TILE, CONV_SUB_TILE)
        x = x_ref[rows, :]
        h = _rms(x, g_ref[0:1, :]).astype(_BF16)
        gate_c = _dot(h, w_in_ref[:, D_MODEL:2 * D_MODEL])
        u = _dot(h, w_in_ref[:, 2 * D_MODEL:3 * D_MODEL])
        gate_b = _dot(h, w_in_ref[:, 0:D_MODEL])
        cu = gate_c * u

        def taps(cur, prev1, prev2):
            return (w_conv_ref[0:1, :] * prev2 + w_conv_ref[1:2, :] * prev1
                    + w_conv_ref[2:3, :] * cur)

        z = taps(cu, pltpu.roll(cu, 1, axis=0), pltpu.roll(cu, 2, axis=0))
        top = cu[0:SUBLANES, :]
        top1 = jnp.where(row == 0, tail[SUBLANES - 1:SUBLANES, :], pltpu.roll(top, 1, axis=0))
        top2 = jnp.where(row == 0, tail[SUBLANES - 2:SUBLANES - 1, :],
                         jnp.where(row == 1, tail[SUBLANES - 1:SUBLANES, :],
                                   pltpu.roll(top, 2, axis=0)))
        z = jnp.concatenate([taps(top, top1, top2), z[SUBLANES:, :]], axis=0)
        tail = cu[CONV_SUB_TILE - SUBLANES:, :]
        m = _dot((gate_b * z).astype(_BF16), w_out_ref[...])
        o_ref[rows, :] = x + _rms(m, g_ref[1:2, :])
    tail_ref[...] = tail


def _conv_mixer(x, g2, w_in, w_conv, w_out, layer, seq):
    t = x.shape[0]
    row_spec = pl.BlockSpec((CONV_ROW_TILE, D_MODEL), lambda i: (i, 0))
    return pl.pallas_call(
        functools.partial(_conv_kernel, seq // CONV_ROW_TILE),
        out_shape=jax.ShapeDtypeStruct(x.shape, _F32),
        grid=(t // CONV_ROW_TILE,),
        in_specs=[row_spec, _resident((2, D_MODEL), lambda i: (0, 0)),
                  _resident((D_MODEL, 3 * D_MODEL), lambda i: (0, 0)),
                  _resident((None, CONV_WIDTH, D_MODEL), lambda i: (layer, 0, 0)),
                  _resident((D_MODEL, D_MODEL), lambda i: (0, 0))],
        out_specs=row_spec,
        scratch_shapes=[pltpu.VMEM((SUBLANES, D_MODEL), _F32)],
        compiler_params=_params(("arbitrary",)),
        name="conv_mixer",
    )(x, g2, w_in, w_conv, w_out)


def _kv_kernel(x_ref, g_ref, w_ref, k_ref, vt_ref):
    ones = jnp.ones((V_ROWS - HEAD_WIDTH, KV_TILE), _BF16)
    for r in range(ROW_TILE // KV_TILE):
        rows = pl.ds(r * KV_TILE, KV_TILE)
        h = _rms(x_ref[rows, :], g_ref[...]).astype(_BF16)
        k_ref[rows, :] = _dot(h, w_ref[:, 0:D_MODEL]).astype(_BF16)
        v_t = _dot(h, w_ref[:, D_MODEL:2 * D_MODEL]).T.astype(_BF16)
        for head in range(N_HEADS):
            vt_ref[r, head, 0:HEAD_WIDTH, :] = v_t[head * HEAD_WIDTH:(head + 1) * HEAD_WIDTH, :]
            vt_ref[r, head, HEAD_WIDTH:V_ROWS, :] = ones


def _shared_kv(x, g_kv, w_kv):
    t = x.shape[0]
    per_step = ROW_TILE // KV_TILE
    return pl.pallas_call(
        _kv_kernel,
        out_shape=(jax.ShapeDtypeStruct((t, D_MODEL), _BF16),
                   jax.ShapeDtypeStruct((t // KV_TILE, N_HEADS, V_ROWS, KV_TILE), _BF16)),
        grid=(t // ROW_TILE,),
        in_specs=[pl.BlockSpec((ROW_TILE, D_MODEL), lambda i: (i, 0)),
                  _resident((1, D_MODEL), lambda i: (0, 0)),
                  _resident((D_MODEL, 2 * D_MODEL), lambda i: (0, 0))],
        out_specs=(pl.BlockSpec((ROW_TILE, D_MODEL), lambda i: (i, 0)),
                   pl.BlockSpec((per_step, N_HEADS, V_ROWS, KV_TILE), lambda i: (i, 0, 0, 0))),
        compiler_params=_params(("parallel",)),
        name="shared_kv",
    )(x, g_kv, w_kv)


def _attn_kernel(lambda_init, n_tiles, x_ref, g_ref, wq_ref, lam_ref, gs_ref, wo_ref, k_ref, vt_ref,
                 o_ref, qb_ref, m_ref, acc_ref, heads_ref):
    p = pl.program_id(1)
    x = x_ref[...]
    h = _rms(x, g_ref[0:1, :]).astype(_BF16)
    q_scale = HEAD_DIM ** -0.5 * math.log2(math.e)
    q_t = (_dot(h, wq_ref[...]) * q_scale).T.astype(_BF16)

    feat = lax.broadcasted_iota(jnp.int32, (HEAD_WIDTH, Q_TILE), 0)
    zero = jnp.zeros((), _BF16)
    for side in range(2):
        for head in range(N_HEADS):
            q_h = q_t[head * HEAD_WIDTH:(head + 1) * HEAD_WIDTH, side * Q_TILE:(side + 1) * Q_TILE]
            qb_ref[side * N_HEADS + head, :, 0:Q_TILE] = jnp.where(feat < HEAD_DIM, q_h, zero)
            qb_ref[side * N_HEADS + head, :, Q_TILE:2 * Q_TILE] = jnp.where(feat >= HEAD_DIM, q_h, zero)

    key_i = lax.broadcasted_iota(jnp.int32, (KV_TILE, 2 * Q_TILE), 0)
    qry_i = lax.broadcasted_iota(jnp.int32, (KV_TILE, 2 * Q_TILE), 1) % Q_TILE
    visible = key_i <= qry_i

    blocks = [(0, p, True), (1, n_tiles - 1 - p, True)]
    for u in range(n_tiles - 1):
        first_side = u < p
        blocks.append((jnp.where(first_side, 0, 1), jnp.where(first_side, u, u - p), False))
    items = [(side, j, diagonal, head) for side, j, diagonal in blocks for head in range(N_HEADS)]

    def scores(item):
        side, j, _, head = item
        k_rows = pl.ds(pl.multiple_of(j * KV_TILE, KV_TILE), KV_TILE)
        k_blk = k_ref[k_rows, head * HEAD_WIDTH:(head + 1) * HEAD_WIDTH]
        return _dot(k_blk, qb_ref[side * N_HEADS + head])

    def fold(item, e, alpha):
        side, j, diagonal, head = item
        pv = _dot(vt_ref[j, head], e)
        state = side * N_HEADS + head
        acc_ref[state] = pv if diagonal else alpha * acc_ref[state] + pv

    ahead = [scores(item) for item in items[:QK_LOOKAHEAD]]
    pending = []
    for n, item in enumerate(items):
        side, j, diagonal, head = item
        state = side * N_HEADS + head
        s = ahead.pop(0)
        if n + QK_LOOKAHEAD < len(items):
            ahead.append(scores(items[n + QK_LOOKAHEAD]))
        alpha = None
        if diagonal:
            s = jnp.where(visible, s, -jnp.inf)
            m_new = jnp.max(s, axis=0, keepdims=True)
        else:
            m_old = m_ref[state]
            m_new = jnp.maximum(m_old, jnp.max(s, axis=0, keepdims=True))
            alpha = jnp.exp2(m_old - m_new)
        m_ref[state] = m_new
        e = jnp.exp2(s - m_new).astype(_BF16)
        pending.append((item, e, alpha))
        if len(pending) > PV_DELAY:
            fold(*pending.pop(0))
    for work in pending:
        fold(*work)

    lam_p = lam_ref[...]
    lam = (jnp.exp(jnp.sum(lam_p[0:1] * lam_p[1:2], axis=-1, keepdims=True))
           - jnp.exp(jnp.sum(lam_p[2:3] * lam_p[3:4], axis=-1, keepdims=True))
           + lambda_init)
    gain = gs_ref[...] * (1.0 - lambda_init)
    for side in range(2):
        for head in range(N_HEADS):
            acc = acc_ref[side * N_HEADS + head]
            inv_l = 1.0 / acc[HEAD_WIDTH:HEAD_WIDTH + 1, :]
            o_h = (acc[0:HEAD_WIDTH, :Q_TILE] * inv_l[:, :Q_TILE]
                   - acc[0:HEAD_WIDTH, Q_TILE:] * (lam * inv_l[:, Q_TILE:]))
            o_h = o_h * lax.rsqrt(jnp.mean(o_h * o_h, axis=0, keepdims=True) + EPS) * gain
            heads_ref[side * Q_TILE:(side + 1) * Q_TILE,
                      head * HEAD_WIDTH:(head + 1) * HEAD_WIDTH] = o_h.T.astype(_BF16)

    mixed = _dot(heads_ref[...], wo_ref[...])
    o_ref[...] = x + _rms(mixed, g_ref[1:2, :])


def _diff_attention(x, g2, w_q, lam_params, g_subln_col, w_o, k, vt, layer, j, batch, seq):
    lambda_init = 0.8 - 0.6 * math.exp(-0.3 * layer)
    n_tiles = seq // Q_TILE
    pairs = n_tiles // 2
    kv_tiles = seq // KV_TILE
    row_spec = pl.BlockSpec((2 * Q_TILE, D_MODEL), lambda b, i: (b * pairs + i, 0))
    return pl.pallas_call(
        functools.partial(_attn_kernel, lambda_init, n_tiles),
        out_shape=jax.ShapeDtypeStruct(x.shape, _F32),
        grid=(batch, pairs),
        in_specs=[row_spec,
                  _resident((2, D_MODEL), lambda b, i: (0, 0)),
                  _resident((D_MODEL, D_MODEL), lambda b, i: (0, 0)),
                  _resident((None, 4, HEAD_DIM), lambda b, i: (j, 0, 0)),
                  _resident((None, HEAD_WIDTH, 1), lambda b, i: (j, 0, 0)),
                  _resident((D_MODEL, D_MODEL), lambda b, i: (0, 0)),
                  pl.BlockSpec((seq, D_MODEL), lambda b, i: (b, 0)),
                  pl.BlockSpec((kv_tiles, N_HEADS, V_ROWS, KV_TILE), lambda b, i: (b, 0, 0, 0))],
        out_specs=row_spec,
        scratch_shapes=[pltpu.VMEM((2 * N_HEADS, HEAD_WIDTH, 2 * Q_TILE), _BF16),
                        pltpu.VMEM((2 * N_HEADS, 1, 2 * Q_TILE), _F32),
                        pltpu.VMEM((2 * N_HEADS, V_ROWS, 2 * Q_TILE), _F32),
                        pltpu.VMEM((2 * Q_TILE, D_MODEL), _BF16)],
        compiler_params=_params(("parallel", "parallel")),
        name="diff_attention",
    )(x, g2, w_q, lam_params, g_subln_col, w_o, k, vt)


def kernel(x, g_norm, w_ffn_gate, w_ffn_up, w_ffn_down, w_conv_in, w_conv, w_conv_out, g_kv, w_kv,
           w_q, lambda_q1, lambda_k1, lambda_q2, lambda_k2, g_subln, w_o):
    batch, seq, d = x.shape
    assert d == D_MODEL and seq % (2 * Q_TILE) == 0 and Q_TILE == KV_TILE
    assert FFN_ROW_TILE % Q_TILE == 0 and (seq // Q_TILE) % (FFN_ROW_TILE // Q_TILE) == 0
    assert seq % ROW_TILE == 0 and seq % CONV_ROW_TILE == 0 and (batch * seq) % FFN_ROW_TILE == 0
    assert sum(w for _, w in FF_CHUNKS) == D_FF

    lam_params = jnp.stack([lambda_q1, lambda_k1, lambda_q2, lambda_k2], axis=1)
    g_subln_col = g_subln[:, :, None]

    def ffn_casts(layer, j):
        return [(w, (layer, j)) for w in (w_ffn_gate, w_ffn_up, w_ffn_down)]

    def mixer_casts(layer):
        if layer < N_A_LAYERS:
            return [(w_conv_in, (layer,)), (w_conv_out, (layer,))]
        return [(w_q, (layer - N_A_LAYERS,)), (w_o, (layer - N_A_LAYERS,))]

    ffn_w = [w[0, 0].astype(_BF16) for w in (w_ffn_gate, w_ffn_up, w_ffn_down)]
    x = x.reshape(batch * seq, d)
    k = vt = None
    for layer in range(DEPTH):
        g = g_norm[layer]
        gather = "to_pairs" if layer == N_A_LAYERS else None
        x, converted = _ffn(x, g[0:2], *ffn_w, casts=ffn_casts(layer, 1) + mixer_casts(layer),
                            gather=gather, seq=seq)
        ffn_w, (w_a, w_b) = converted[:3], converted[3:]
        if layer < N_A_LAYERS:
            x = _conv_mixer(x, g[2:4], w_a, w_conv, w_b, layer, seq)
        else:
            x = _diff_attention(x, g[2:4], w_a, lam_params, g_subln_col, w_b, k, vt,
                                layer, layer - N_A_LAYERS, batch, seq)
        casts = ffn_casts(layer + 1, 0) if layer + 1 < DEPTH else []
        if layer == N_A_LAYERS - 1:
            casts = casts + [(w_kv, ())]
        gather = "from_pairs" if layer == DEPTH - 1 else None
        x, converted = _ffn(x, g[4:6], *ffn_w, casts=casts, gather=gather, seq=seq)
        ffn_w = converted[:3]
        if layer == N_A_LAYERS - 1:
            k, vt = _shared_kv(x, g_kv.reshape(1, d), converted[3])
    return x.reshape(batch, seq, d)
```
